```python
import jax, jax.numpy as jnp
from jax import lax
import numpy as np

D_MODEL = 4096
BATCH = 2
SEQ = 8192
DEPTH = 2

GRID_W = 64
D_MIX = D_MODEL
HEAD_DIM = 128
D_ATTN = D_MIX // 2
N_Q_HEADS = D_ATTN // HEAD_DIM
N_KV_HEADS = 4
Q_PER_KV = N_Q_HEADS // N_KV_HEADS
D_KV = N_KV_HEADS * HEAD_DIM
D_RNN = D_MIX - D_ATTN
RNN_BLOCK = 128
N_RNN_BLOCKS = D_RNN // RNN_BLOCK
D_IN = D_ATTN + 2 * D_KV + 2 * D_RNN
CONV_W = 4
RG_C = 8.0
ROPE_THETA = 10000.0
ROPE_AXIS_DIM = HEAD_DIM // 2
Q_BLOCK = 128
D_FF = (-(-8 * D_MODEL // (3 * 256))) * 256
ALPHA = (2 * DEPTH) ** 0.25
BETA = (8 * DEPTH) ** -0.25
LN_EPS = 1e-5
RMS_EPS = 1e-6

kernel_name = "hybrid_gqa_rglru_deepnorm_adaln_encoder"


def _rms_norm_f32(x, g):
    xf = x.astype(jnp.float32)
    return xf * lax.rsqrt(jnp.mean(jnp.square(xf), axis=-1, keepdims=True) + RMS_EPS) * g.astype(jnp.float32)


def _deepnorm_residual(x, y, gate, g, b):
    s = ALPHA * x.astype(jnp.float32) + gate.astype(jnp.float32) * y.astype(jnp.float32)
    mu = jnp.mean(s, axis=-1, keepdims=True)
    sc = s - mu
    var = jnp.mean(jnp.square(sc), axis=-1, keepdims=True)
    out = sc * lax.rsqrt(var + LN_EPS) * g.astype(jnp.float32) + b.astype(jnp.float32)
    return out.astype(x.dtype)


def _axial_rope_tables(seq):
    rows = seq // GRID_W
    row_idx, col_idx = jnp.meshgrid(jnp.arange(rows, dtype=jnp.float32),
                                    jnp.arange(GRID_W, dtype=jnp.float32), indexing="ij")
    row_idx = row_idx.reshape(-1)
    col_idx = col_idx.reshape(-1)
    inv_freq = jnp.power(ROPE_THETA, -jnp.arange(0, ROPE_AXIS_DIM, 2, dtype=jnp.float32) / ROPE_AXIS_DIM)
    ang_r = row_idx[:, None] * inv_freq[None, :]
    ang_c = col_idx[:, None] * inv_freq[None, :]
    return (jnp.cos(ang_r), jnp.sin(ang_r), jnp.cos(ang_c), jnp.sin(ang_c))


def _rotate(x, cos, sin):
    x1, x2 = jnp.split(x, 2, axis=-1)
    cos = cos[None, :, None, :]
    sin = sin[None, :, None, :]
    return jnp.concatenate([x1 * cos - x2 * sin, x2 * cos + x1 * sin], axis=-1)


def _axial_rope(x, tables):
    cos_r, sin_r, cos_c, sin_c = tables
    return jnp.concatenate([_rotate(x[..., :ROPE_AXIS_DIM], cos_r, sin_r),
                            _rotate(x[..., ROPE_AXIS_DIM:], cos_c, sin_c)], axis=-1)


def _grid_attention(q, k, v, q_g, k_g, tables):
    bsz, seq = q.shape[0], q.shape[1]
    dt = v.dtype
    q = q.reshape(bsz, seq, N_Q_HEADS, HEAD_DIM)
    k = k.reshape(bsz, seq, N_KV_HEADS, HEAD_DIM)
    v = v.reshape(bsz, seq, N_KV_HEADS, HEAD_DIM)
    q = (_axial_rope(_rms_norm_f32(q, q_g), tables) * HEAD_DIM ** -0.5).astype(dt)
    k = _axial_rope(_rms_norm_f32(k, k_g), tables).astype(dt)
    n_blk = seq // Q_BLOCK
    q_blocks = q.reshape(bsz, n_blk, Q_BLOCK, N_KV_HEADS, Q_PER_KV, HEAD_DIM).transpose(1, 0, 2, 3, 4, 5)

    def attend(q_blk):
        s = jnp.einsum("bqkgd,bskd->bkgqs", q_blk, k, preferred_element_type=jnp.float32)
        p = jax.nn.softmax(s, axis=-1).astype(dt)
        return jnp.einsum("bkgqs,bskd->bqkgd", p, v)

    o = lax.map(attend, q_blocks)
    return o.transpose(1, 0, 2, 3, 4, 5).reshape(bsz, seq, D_ATTN)


def _centred_depthwise_conv(x, w, b):
    pad_lo = (CONV_W - 1) // 2
    y = lax.conv_general_dilated(x, w[:, None, :], window_strides=(1,),
                                 padding=[(pad_lo, CONV_W - 1 - pad_lo)],
                                 dimension_numbers=("NWC", "WIO", "NWC"),
                                 feature_group_count=x.shape[-1])
    return y + b


def _linear_scan(a, u, reverse):
    def combine(e1, e2):
        a1, b1 = e1
        a2, b2 = e2
        return a1 * a2, a2 * b1 + b2
    _, h = lax.associative_scan(combine, (a, u), reverse=reverse, axis=1)
    return h


def _bidir_rg_lru(x, w_a, b_a, w_i, b_i, lam):
    bsz, seq = x.shape[0], x.shape[1]
    xb = x.reshape(bsz, seq, N_RNN_BLOCKS, RNN_BLOCK)

    def block_diag_gate(w, b):
        y = jnp.einsum("bshi,nhij->nbshj", xb, w, preferred_element_type=jnp.float32)
        return jax.nn.sigmoid(y.reshape(2, bsz, seq, D_RNN) + b.astype(jnp.float32)[:, None, None, :])

    r = block_diag_gate(w_a, b_a)
    i = block_diag_gate(w_i, b_i)
    log_a = -RG_C * jax.nn.softplus(-lam.astype(jnp.float32))[:, None, None, :] * r
    a = jnp.exp(log_a)
    u = jnp.sqrt(-jnp.expm1(2.0 * log_a)) * i * x.astype(jnp.float32)[None]
    return _linear_scan(a[0], u[0], False) + _linear_scan(a[1], u[1], True)


def _hybrid_mixer(u, w_in, qk_g, conv_w, conv_b, w_a, b_a, w_i, b_i, lam, out_g, w_o, tables):
    z = u @ w_in
    cuts = (D_ATTN, D_ATTN + D_KV, D_ATTN + 2 * D_KV, D_ATTN + 2 * D_KV + D_RNN)
    q, k, v, x_rnn, x_gate = jnp.split(z, cuts, axis=-1)
    y_attn = _grid_attention(q, k, v, qk_g[0], qk_g[1], tables)
    h = _bidir_rg_lru(_centred_depthwise_conv(x_rnn, conv_w, conv_b), w_a, b_a, w_i, b_i, lam)
    y_rnn = jax.nn.gelu(x_gate.astype(jnp.float32)) * h
    y = jnp.concatenate([_rms_norm_f32(y_attn, out_g[:D_ATTN]),
                         _rms_norm_f32(y_rnn, out_g[D_ATTN:])], axis=-1)
    return y.astype(u.dtype) @ w_o


def _swiglu(u, w_in, w_down):
    gate, up = jnp.split(u @ w_in, 2, axis=-1)
    return (jax.nn.silu(gate) * up) @ w_down


def setup_inputs(seed: int = 0) -> dict:
    key = jax.random.key(seed)
    ks = jax.random.split(key, 20)

    def nrm(k, shape, scale):
        return jax.random.normal(k, shape, jnp.float32) * scale

    x = nrm(ks[0], (BATCH, SEQ, D_MODEL), 1.0)
    c = nrm(ks[1], (BATCH, D_MODEL), 1.0)
    w_ada = nrm(ks[2], (DEPTH, D_MODEL, 6 * D_MODEL), D_MODEL ** -0.5)
    b_ada = nrm(ks[3], (DEPTH, 6 * D_MODEL), 0.02)
    w_in = nrm(ks[4], (DEPTH, D_MODEL, D_IN), D_MODEL ** -0.5)
    qk_norm_g = 1.0 + nrm(ks[5], (DEPTH, 2, HEAD_DIM), 0.02)
    conv_w = nrm(ks[6], (DEPTH, CONV_W, D_RNN), CONV_W ** -0.5)
    conv_b = nrm(ks[7], (DEPTH, D_RNN), 0.02)
    rg_w_a = nrm(ks[8], (DEPTH, 2, N_RNN_BLOCKS, RNN_BLOCK, RNN_BLOCK), RNN_BLOCK ** -0.5)
    rg_b_a = nrm(ks[9], (DEPTH, 2, D_RNN), 0.02)
    rg_w_i = nrm(ks[10], (DEPTH, 2, N_RNN_BLOCKS, RNN_BLOCK, RNN_BLOCK), RNN_BLOCK ** -0.5)
    rg_b_i = nrm(ks[11], (DEPTH, 2, D_RNN), 0.02)
    a_target = jax.random.uniform(ks[12], (DEPTH, 2, D_RNN), jnp.float32, 0.9, 0.999)
    a_base = a_target ** (1.0 / RG_C)
    rg_lam = jnp.log(a_base) - jnp.log1p(-a_base)
    mix_out_g = 1.0 + nrm(ks[13], (DEPTH, D_MIX), 0.02)
    w_o = nrm(ks[14], (DEPTH, D_MIX, D_MODEL), BETA * D_MIX ** -0.5)
    ln_g = 1.0 + nrm(ks[15], (DEPTH, 2, D_MODEL), 0.02)
    ln_b = nrm(ks[16], (DEPTH, 2, D_MODEL), 0.02)
    w_ffn_in = nrm(ks[17], (DEPTH, D_MODEL, 2 * D_FF), D_MODEL ** -0.5)
    w_down = nrm(ks[18], (DEPTH, D_FF, D_MODEL), BETA * D_FF ** -0.5)
    return {"x": x, "c": c, "w_ada": w_ada, "b_ada": b_ada, "w_in": w_in,
            "qk_norm_g": qk_norm_g, "conv_w": conv_w, "conv_b": conv_b,
            "rg_w_a": rg_w_a, "rg_b_a": rg_b_a, "rg_w_i": rg_w_i, "rg_b_i": rg_b_i,
            "rg_lam": rg_lam, "mix_out_g": mix_out_g, "w_o": w_o, "ln_g": ln_g,
            "ln_b": ln_b, "w_ffn_in": w_ffn_in, "w_down": w_down}


def reference(x, c, w_ada, b_ada, w_in, qk_norm_g, conv_w, conv_b, rg_w_a, rg_b_a, rg_w_i, rg_b_i,
              rg_lam, mix_out_g, w_o, ln_g, ln_b, w_ffn_in, w_down):
    tables = _axial_rope_tables(x.shape[1])
    c_act = jax.nn.silu(c)
    for l in range(DEPTH):
        mod = c_act @ w_ada[l] + b_ada[l]
        sh1, sc1, g1, sh2, sc2, g2 = [m[:, None, :] for m in jnp.split(mod, 6, axis=-1)]
        y = _hybrid_mixer(x * (1.0 + sc1) + sh1, w_in[l], qk_norm_g[l], conv_w[l], conv_b[l],
                          rg_w_a[l], rg_b_a[l], rg_w_i[l], rg_b_i[l], rg_lam[l], mix_out_g[l],
                          w_o[l], tables)
        x = _deepnorm_residual(x, y, g1, ln_g[l, 0], ln_b[l, 0])
        y = _swiglu(x * (1.0 + sc2) + sh2, w_ffn_in[l], w_down[l])
        x = _deepnorm_residual(x, y, g2, ln_g[l, 1], ln_b[l, 1])
    return x
```

```python
import functools
import math

import jax
import jax.numpy as jnp
from jax import lax
from jax.experimental import pallas as pl
from jax.experimental.pallas import tpu as pltpu

D_MODEL = 4096
GRID_W = 64
HEAD_DIM = 128
D_ATTN = D_MODEL // 2
N_Q_HEADS = D_ATTN // HEAD_DIM
N_KV_HEADS = 4
Q_PER_KV = N_Q_HEADS // N_KV_HEADS
D_KV = N_KV_HEADS * HEAD_DIM
D_RNN = D_MODEL - D_ATTN
RNN_BLOCK = 128
N_RNN_BLOCKS = D_RNN // RNN_BLOCK
D_IN = D_ATTN + 2 * D_KV + 2 * D_RNN
CONV_W = 4
RG_C = 8.0
ROPE_THETA = 10000.0
ROPE_AXIS_DIM = HEAD_DIM // 2
D_FF = (-(-8 * D_MODEL // (3 * 256))) * 256
LN_EPS = 1e-5
RMS_EPS = 1e-6

V7X_VMEM_BYTES = 64 * 1024 * 1024
VMEM_LIMIT = V7X_VMEM_BYTES - 8 * 1024 * 1024
LANES = 128
SUBLANES = 8

BF16 = jnp.bfloat16
F32 = jnp.float32

D_FF_PAD = 11264
ADA_TN = 1024
INPROJ_TM = 512
INPROJ_TN = 1024
ATTN_TQ = 256
ATTN_TK = 1024
MM_TM = 1024
MM_TN = 1024
FFN_TN = 512
DOWN_TM = 512
DOWN_TK = 1024
ROW_TM = 256
RNN_CHUNK = 512
RNN_CHAINS = 4


def _params(sem):
    return pltpu.CompilerParams(dimension_semantics=sem, vmem_limit_bytes=VMEM_LIMIT)


def _ada_kernel(c_ref, w_ref, b_ref, o_ref):
    c = c_ref[...]
    c_act = (c * jax.nn.sigmoid(c)).astype(BF16)
    o_ref[...] = jnp.dot(c_act, w_ref[...].astype(BF16), preferred_element_type=F32) + b_ref[...]


def _ada(c_pad, w_ada, b_ada):
    depth, d, n = w_ada.shape
    return pl.pallas_call(
        _ada_kernel,
        out_shape=jax.ShapeDtypeStruct((depth, SUBLANES, n), F32),
        grid=(depth, n // ADA_TN),
        in_specs=[
            pl.BlockSpec((SUBLANES, d), lambda l, j: (0, 0)),
            pl.BlockSpec((None, d, ADA_TN), lambda l, j: (l, 0, j)),
            pl.BlockSpec((None, 1, ADA_TN), lambda l, j: (l, 0, j)),
        ],
        out_specs=pl.BlockSpec((None, SUBLANES, ADA_TN), lambda l, j: (l, 0, j)),
        compiler_params=_params(("arbitrary", "arbitrary")),
        name="ada_mod",
    )(c_pad, w_ada, b_ada.reshape(depth, 1, n))


def _modulate_kernel(x_ref, sc_ref, sh_ref, u_ref):
    u_ref[...] = (x_ref[...] * (1.0 + sc_ref[...]) + sh_ref[...]).astype(BF16)


def _modulate(x2, sc, sh, seq):
    m, d = x2.shape
    per_b = seq // ROW_TM
    vec = pl.BlockSpec((None, 1, d), lambda i: (i // per_b, 0, 0))
    return pl.pallas_call(
        _modulate_kernel,
        out_shape=jax.ShapeDtypeStruct((m, d), BF16),
        grid=(m // ROW_TM,),
        in_specs=[pl.BlockSpec((ROW_TM, d), lambda i: (i, 0)), vec, vec],
        out_specs=pl.BlockSpec((ROW_TM, d), lambda i: (i, 0)),
        compiler_params=_params(("arbitrary",)),
        name="modulate",
    )(x2, sc, sh)


def _rms_rope(xh, gain, cos, sin_a, sin_b):
    r = lax.rsqrt(jnp.mean(xh * xh, axis=-1, keepdims=True) + RMS_EPS)
    xn = xh * r * gain
    return xn * cos + pltpu.roll(xn, 96, 1) * sin_a + pltpu.roll(xn, 32, 1) * sin_b


def _inproj_kernel(u_ref, w_ref, g_ref, cos_ref, sa_ref, sb_ref,
                   q_ref, k_ref, vt_ref, xr_ref, xg_ref):
    j = pl.program_id(1)
    z = jnp.dot(u_ref[...], w_ref[...], preferred_element_type=F32)
    heads_per_tile = INPROJ_TN // HEAD_DIM

    @pl.when(j < 2)
    def _():
        cos, sa, sb = cos_ref[...], sa_ref[...], sb_ref[...]
        gain = g_ref[0:1, :]
        for h in range(heads_per_tile):
            xh = z[:, h * HEAD_DIM:(h + 1) * HEAD_DIM]
            q = _rms_rope(xh, gain, cos, sa, sb) * (HEAD_DIM ** -0.5)
            q_ref[h] = q.astype(BF16)

    @pl.when(j == 2)
    def _():
        cos, sa, sb = cos_ref[...], sa_ref[...], sb_ref[...]
        gain = g_ref[1:2, :]
        for h in range(N_KV_HEADS):
            xh = z[:, h * HEAD_DIM:(h + 1) * HEAD_DIM]
            k_ref[h] = _rms_rope(xh, gain, cos, sa, sb).astype(BF16)
            vh = z[:, D_KV + h * HEAD_DIM:D_KV + (h + 1) * HEAD_DIM]
            vt_ref[h] = vh.T.astype(BF16)

    @pl.when((j == 3) | (j == 4))
    def _():
        xr_ref[...] = z

    @pl.when(j >= 5)
    def _():
        xg_ref[...] = z


def _inproj(u, w_in, qk_g, cos, sin_a, sin_b, seq):
    m, d = u.shape
    tm, tn = INPROJ_TM, INPROJ_TN
    per_b = seq // tm
    tab = pl.BlockSpec((tm, HEAD_DIM), lambda i, j: (i % per_b, 0))
    q_tiles = D_ATTN // tn
    return pl.pallas_call(
        _inproj_kernel,
        out_shape=(
            jax.ShapeDtypeStruct((N_Q_HEADS, m, HEAD_DIM), BF16),
            jax.ShapeDtypeStruct((N_KV_HEADS, m, HEAD_DIM), BF16),
            jax.ShapeDtypeStruct((N_KV_HEADS, HEAD_DIM, m), BF16),
            jax.ShapeDtypeStruct((m, D_RNN), F32),
            jax.ShapeDtypeStruct((m, D_RNN), F32),
        ),
        grid=(m // tm, D_IN // tn),
        in_specs=[
            pl.BlockSpec((tm, d), lambda i, j: (i, 0)),
            pl.BlockSpec((d, tn), lambda i, j: (0, j)),
            pl.BlockSpec((2, HEAD_DIM), lambda i, j: (0, 0)),
            tab, tab, tab,
        ],
        out_specs=(
            pl.BlockSpec((tn // HEAD_DIM, tm, HEAD_DIM),
                         lambda i, j: (jnp.minimum(j, q_tiles - 1), i, 0)),
            pl.BlockSpec((N_KV_HEADS, tm, HEAD_DIM), lambda i, j: (0, i, 0)),
            pl.BlockSpec((N_KV_HEADS, HEAD_DIM, tm), lambda i, j: (0, 0, i)),
            pl.BlockSpec((tm, tn), lambda i, j: (i, jnp.clip(j - 3, 0, 1))),
            pl.BlockSpec((tm, tn), lambda i, j: (i, jnp.clip(j - 5, 0, 1))),
        ),
        compiler_params=_params(("arbitrary", "arbitrary")),
        name="in_proj",
    )(u, w_in, qk_g, cos, sin_a, sin_b)


def _attn_kernel(q_ref, k_ref, vt_ref, g_ref, o_ref, m_s, l_s, acc_s):
    j = pl.program_id(2)
    tq = q_ref.shape[1]

    @pl.when(j == 0)
    def _():
        m_s[...] = jnp.full(m_s.shape, -jnp.inf, F32)
        l_s[...] = jnp.zeros(l_s.shape, F32)
        acc_s[...] = jnp.zeros(acc_s.shape, F32)

    for kh in range(N_KV_HEADS):
        q4 = q_ref[Q_PER_KV * kh:Q_PER_KV * (kh + 1)].reshape(Q_PER_KV * tq, HEAD_DIM)
        s_t = lax.dot_general(k_ref[kh], q4, (((1,), (1,)), ((), ())),
                              preferred_element_type=F32)
        m_old = m_s[kh:kh + 1, :]
        m_new = jnp.maximum(m_old, jnp.max(s_t, axis=0, keepdims=True))
        alpha = jnp.exp(m_old - m_new)
        p = jnp.exp(s_t - m_new)
        l_s[kh:kh + 1, :] = alpha * l_s[kh:kh + 1, :] + jnp.sum(p, axis=0, keepdims=True)
        pv = jnp.dot(vt_ref[kh], p.astype(BF16), preferred_element_type=F32)
        acc_s[kh] = alpha * acc_s[kh] + pv
        m_s[kh:kh + 1, :] = m_new

    @pl.when(j == pl.num_programs(2) - 1)
    def _():
        outs = []
        ss = jnp.zeros((1, tq), F32)
        for kh in range(N_KV_HEADS):
            o = acc_s[kh] / l_s[kh:kh + 1, :]
            sq = jnp.sum(o * o, axis=0, keepdims=True)
            for g in range(Q_PER_KV):
                ss = ss + sq[:, g * tq:(g + 1) * tq]
            outs.append(o)
        r = lax.rsqrt(ss * (1.0 / D_ATTN) + RMS_EPS)
        for kh in range(N_KV_HEADS):
            for g in range(Q_PER_KV):
                h = kh * Q_PER_KV + g
                oh = (outs[kh][:, g * tq:(g + 1) * tq] * r).T
                o_ref[:, h * HEAD_DIM:(h + 1) * HEAD_DIM] = (
                    oh * g_ref[:, h * HEAD_DIM:(h + 1) * HEAD_DIM]).astype(BF16)


def _attention(q, k, vt, out_g_attn, batch, seq):
    tq, tk = ATTN_TQ, ATTN_TK
    nq, nk = seq // tq, seq // tk
    m = batch * seq
    return pl.pallas_call(
        _attn_kernel,
        out_shape=jax.ShapeDtypeStruct((m, D_ATTN), BF16),
        grid=(batch, nq, nk),
        in_specs=[
            pl.BlockSpec((N_Q_HEADS, tq, HEAD_DIM), lambda b, i, j: (0, b * nq + i, 0)),
            pl.BlockSpec((N_KV_HEADS, tk, HEAD_DIM), lambda b, i, j: (0, b * nk + j, 0)),
            pl.BlockSpec((N_KV_HEADS, HEAD_DIM, tk), lambda b, i, j: (0, 0, b * nk + j)),
            pl.BlockSpec((1, D_ATTN), lambda b, i, j: (0, 0)),
        ],
        out_specs=pl.BlockSpec((tq, D_ATTN), lambda b, i, j: (b * nq + i, 0)),
        scratch_shapes=[
            pltpu.VMEM((SUBLANES, Q_PER_KV * tq), F32),
            pltpu.VMEM((SUBLANES, Q_PER_KV * tq), F32),
            pltpu.VMEM((N_KV_HEADS, HEAD_DIM, Q_PER_KV * tq), F32),
        ],
        compiler_params=_params(("arbitrary", "arbitrary", "arbitrary")),
        name="attention",
    )(q, k, vt, out_g_attn)


def _gelu_tanh(x):
    return 0.5 * x * (1.0 + jnp.tanh(math.sqrt(2.0 / math.pi) * (x + 0.044715 * (x * x * x))))


def _softplus(x):
    return jnp.maximum(x, 0.0) + jnp.log1p(jnp.exp(-jnp.abs(x)))


def _rglru_kernel(xr_ref, xg_ref, cw_ref, cb_ref, wa_ref, ba_ref, wi_ref, bi_ref, lam_ref,
                  y_ref, a_s, u_s):
    seq = xr_ref.shape[0]
    t = RNN_CHUNK
    n_chunks = seq // t
    seg = seq // (SUBLANES * RNN_CHAINS)

    cw = cw_ref[...]
    cb = cb_ref[...]
    coef = -RG_C * _softplus(-lam_ref[...])
    w_a = [wa_ref[d].astype(BF16) for d in range(2)]
    w_i = [wi_ref[d].astype(BF16) for d in range(2)]

    def gates(c, carry):
        t0 = pl.multiple_of(c * t, t)
        prev = xr_ref[pl.ds(pl.multiple_of(jnp.maximum(t0 - SUBLANES, 0), SUBLANES), SUBLANES), :]
        nxt = xr_ref[pl.ds(pl.multiple_of(jnp.minimum(t0 + t, seq - SUBLANES), SUBLANES), SUBLANES), :]
        prev = prev * jnp.where(c > 0, 1.0, 0.0)
        nxt = nxt * jnp.where(c < n_chunks - 1, 1.0, 0.0)
        xe = jnp.concatenate([prev, xr_ref[pl.ds(t0, t), :], nxt], axis=0)
        rows = t + 2 * SUBLANES
        xc = cb + cw[1:2, :] * xe[SUBLANES:SUBLANES + t]
        for tap, off in ((0, -1), (2, 1), (3, 2)):
            shifted = pltpu.roll(xe, (-off) % rows, 0)
            xc = xc + cw[tap:tap + 1, :] * shifted[SUBLANES:SUBLANES + t]
        xcb = xc.astype(BF16)
        for d in range(2):
            r = jax.nn.sigmoid(jnp.dot(xcb, w_a[d], preferred_element_type=F32) + ba_ref[d:d + 1, :])
            i = jax.nn.sigmoid(jnp.dot(xcb, w_i[d], preferred_element_type=F32) + bi_ref[d:d + 1, :])
            log_a = coef[d:d + 1, :] * r
            a = jnp.exp(log_a)
            a_s[d, pl.ds(t0, t), :] = a
            u_s[d, pl.ds(t0, t), :] = jnp.sqrt(-jnp.tanh(log_a) * (a * a + 1.0)) * i * xc
        return carry

    lax.fori_loop(0, n_chunks, gates, 0)

    def seg_rows(g, j):
        return pl.ds(g * SUBLANES * seg + j, SUBLANES, stride=seg)

    def local(j, carry):
        out = []
        for d in range(2):
            jj = j if d == 0 else seg - 1 - j
            for g in range(RNN_CHAINS):
                h, p = carry[d * RNN_CHAINS + g]
                a = a_s[d, seg_rows(g, jj), :]
                out.append((a * h + u_s[d, seg_rows(g, jj), :], a * p))
        return tuple(out)

    zero = jnp.zeros((SUBLANES, LANES), F32)
    one = jnp.ones((SUBLANES, LANES), F32)
    ends = lax.fori_loop(0, seg, local, tuple((zero, one) for _ in range(2 * RNN_CHAINS)), unroll=4)

    n_seg = SUBLANES * RNN_CHAINS
    starts = []
    for d in range(2):
        h_end = jnp.concatenate([ends[d * RNN_CHAINS + g][0] for g in range(RNN_CHAINS)], axis=0)
        p_end = jnp.concatenate([ends[d * RNN_CHAINS + g][1] for g in range(RNN_CHAINS)], axis=0)
        order = range(n_seg) if d == 0 else range(n_seg - 1, -1, -1)
        carry = jnp.zeros((1, LANES), F32)
        rows = [None] * n_seg
        for s in order:
            rows[s] = carry
            carry = p_end[s:s + 1, :] * carry + h_end[s:s + 1, :]
        init = jnp.concatenate(rows, axis=0)
        starts.extend(init[g * SUBLANES:(g + 1) * SUBLANES] for g in range(RNN_CHAINS))

    def final(j, carry):
        out = []
        for d in range(2):
            jj = j if d == 0 else seg - 1 - j
            for g in range(RNN_CHAINS):
                h = a_s[d, seg_rows(g, jj), :] * carry[d * RNN_CHAINS + g] + u_s[d, seg_rows(g, jj), :]
                u_s[d, seg_rows(g, jj), :] = h
                out.append(h)
        return tuple(out)

    lax.fori_loop(0, seg, final, tuple(starts), unroll=4)

    def combine(c, carry):
        t0 = pl.multiple_of(c * t, t)
        h = u_s[0, pl.ds(t0, t), :] + u_s[1, pl.ds(t0, t), :]
        y_ref[pl.ds(t0, t), :] = _gelu_tanh(xg_ref[pl.ds(t0, t), :]) * h
        return carry

    lax.fori_loop(0, n_chunks, combine, 0)


def _rglru(xr, xg, conv_w, conv_b, w_a, b_a, w_i, b_i, lam, batch, seq):
    m = batch * seq
    blk = RNN_BLOCK
    slab = pl.BlockSpec((seq, blk), lambda b, n: (b, n))
    vec2 = pl.BlockSpec((2, blk), lambda b, n: (0, n))
    wspec = pl.BlockSpec((2, None, blk, blk), lambda b, n: (0, n, 0, 0))
    return pl.pallas_call(
        _rglru_kernel,
        out_shape=jax.ShapeDtypeStruct((m, D_RNN), F32),
        grid=(batch, N_RNN_BLOCKS),
        in_specs=[
            slab, slab,
            pl.BlockSpec((CONV_W, blk), lambda b, n: (0, n)),
            pl.BlockSpec((1, blk), lambda b, n: (0, n)),
            wspec, vec2, wspec, vec2, vec2,
        ],
        out_specs=slab,
        scratch_shapes=[pltpu.VMEM((2, seq, blk), F32), pltpu.VMEM((2, seq, blk), F32)],
        compiler_params=_params(("arbitrary", "arbitrary")),
        name="rg_lru",
    )(xr, xg, conv_w, conv_b.reshape(1, D_RNN), w_a, b_a, w_i, b_i, lam)


def _rms_rows_kernel(y_ref, g_ref, o_ref):
    y = y_ref[...]
    r = lax.rsqrt(jnp.mean(y * y, axis=-1, keepdims=True) + RMS_EPS)
    o_ref[...] = (y * r * g_ref[...]).astype(BF16)


def _rms_rows(y, gain):
    m, d = y.shape
    return pl.pallas_call(
        _rms_rows_kernel,
        out_shape=jax.ShapeDtypeStruct((m, d), BF16),
        grid=(m // ROW_TM,),
        in_specs=[pl.BlockSpec((ROW_TM, d), lambda i: (i, 0)), pl.BlockSpec((1, d), lambda i: (0, 0))],
        out_specs=pl.BlockSpec((ROW_TM, d), lambda i: (i, 0)),
        compiler_params=_params(("arbitrary",)),
        name="rnn_rms",
    )(y, gain)


def _outproj_kernel(a1_ref, a2_ref, w_ref, o_ref):
    k1 = a1_ref.shape[1]
    o_ref[...] = (jnp.dot(a1_ref[...], w_ref[0:k1, :], preferred_element_type=F32)
                  + jnp.dot(a2_ref[...], w_ref[k1:, :], preferred_element_type=F32))


def _outproj(a1, a2, w_o):
    m, k1 = a1.shape
    k2 = a2.shape[1]
    n = w_o.shape[1]
    return pl.pallas_call(
        _outproj_kernel,
        out_shape=jax.ShapeDtypeStruct((m, n), F32),
        grid=(m // MM_TM, n // MM_TN),
        in_specs=[
            pl.BlockSpec((MM_TM, k1), lambda i, j: (i, 0)),
            pl.BlockSpec((MM_TM, k2), lambda i, j: (i, 0)),
            pl.BlockSpec((k1 + k2, MM_TN), lambda i, j: (0, j)),
        ],
        out_specs=pl.BlockSpec((MM_TM, MM_TN), lambda i, j: (i, j)),
        compiler_params=_params(("arbitrary", "arbitrary")),
        name="out_proj",
    )(a1, a2, w_o)


def _deepnorm_kernel(x_ref, y_ref, gate_ref, g_ref, b_ref, *rest, alpha, modulate):
    s = alpha * x_ref[...] + gate_ref[...] * y_ref[...]
    mu = jnp.mean(s, axis=-1, keepdims=True)
    sc = s - mu
    var = jnp.mean(sc * sc, axis=-1, keepdims=True)
    out = sc * lax.rsqrt(var + LN_EPS) * g_ref[...] + b_ref[...]
    if modulate:
        sc_ref, sh_ref, o_ref, u_ref = rest
        u_ref[...] = (out * (1.0 + sc_ref[...]) + sh_ref[...]).astype(BF16)
    else:
        (o_ref,) = rest
    o_ref[...] = out


def _deepnorm(x2, y, gate, ln_g, ln_b, nxt, alpha, seq):
    m, d = x2.shape
    per_b = seq // ROW_TM
    rows = pl.BlockSpec((ROW_TM, d), lambda i: (i, 0))
    vec_b = pl.BlockSpec((None, 1, d), lambda i: (i // per_b, 0, 0))
    vec = pl.BlockSpec((1, d), lambda i: (0, 0))
    modulate = nxt is not None
    in_specs = [rows, rows, vec_b, vec, vec]
    args = [x2, y, gate, ln_g.reshape(1, d), ln_b.reshape(1, d)]
    out_shape = [jax.ShapeDtypeStruct((m, d), F32)]
    out_specs = [rows]
    if modulate:
        in_specs += [vec_b, vec_b]
        args += list(nxt)
        out_shape.append(jax.ShapeDtypeStruct((m, d), BF16))
        out_specs.append(rows)
    res = pl.pallas_call(
        functools.partial(_deepnorm_kernel, alpha=alpha, modulate=modulate),
        out_shape=tuple(out_shape),
        grid=(m // ROW_TM,),
        in_specs=in_specs,
        out_specs=tuple(out_specs),
        compiler_params=_params(("arbitrary",)),
        name="deepnorm",
    )(*args)
    return res if modulate else (res[0], None)


def _ffn_in_kernel(u_ref, wg_ref, wu_ref, h_ref):
    u = u_ref[...]
    gate = jnp.dot(u, wg_ref[...], preferred_element_type=F32)
    up = jnp.dot(u, wu_ref[...], preferred_element_type=F32)
    h_ref[...] = (gate * jax.nn.sigmoid(gate) * up).astype(BF16)


def _ffn_in(u, w_ffn):
    m, d = u.shape
    nj = D_FF_PAD // FFN_TN
    return pl.pallas_call(
        _ffn_in_kernel,
        out_shape=jax.ShapeDtypeStruct((m, D_FF_PAD), BF16),
        grid=(m // MM_TM, nj),
        in_specs=[
            pl.BlockSpec((MM_TM, d), lambda i, j: (i, 0)),
            pl.BlockSpec((d, FFN_TN), lambda i, j: (0, j)),
            pl.BlockSpec((d, FFN_TN), lambda i, j: (0, j + nj)),
        ],
        out_specs=pl.BlockSpec((MM_TM, FFN_TN), lambda i, j: (i, j)),
        compiler_params=_params(("arbitrary", "arbitrary")),
        name="ffn_in",
    )(u, w_ffn, w_ffn)


def _ffn_down_kernel(h_ref, w_ref, o_ref):
    part = jnp.dot(h_ref[...], w_ref[...], preferred_element_type=F32)

    @pl.when(pl.program_id(1) == 0)
    def _():
        o_ref[...] = part

    @pl.when(pl.program_id(1) > 0)
    def _():
        o_ref[...] += part


def _ffn_down(h, w_down):
    m, kdim = h.shape
    n = w_down.shape[1]
    return pl.pallas_call(
        _ffn_down_kernel,
        out_shape=jax.ShapeDtypeStruct((m, n), F32),
        grid=(m // DOWN_TM, kdim // DOWN_TK),
        in_specs=[
            pl.BlockSpec((DOWN_TM, DOWN_TK), lambda i, k: (i, k)),
            pl.BlockSpec((DOWN_TK, n), lambda i, k: (k, 0)),
        ],
        out_specs=pl.BlockSpec((DOWN_TM, n), lambda i, k: (i, 0)),
        compiler_params=_params(("arbitrary", "arbitrary")),
        name="ffn_down",
    )(h, w_down)


def _rope_tables(seq):
    pos = jnp.arange(seq, dtype=jnp.int32)
    row = (pos // GRID_W).astype(F32)
    col = (pos % GRID_W).astype(F32)
    inv_freq = jnp.power(ROPE_THETA, -jnp.arange(0, ROPE_AXIS_DIM, 2, dtype=F32) / ROPE_AXIS_DIM)
    ang_r = row[:, None] * inv_freq[None, :]
    ang_c = col[:, None] * inv_freq[None, :]
    cos_r, sin_r, cos_c, sin_c = jnp.cos(ang_r), jnp.sin(ang_r), jnp.cos(ang_c), jnp.sin(ang_c)
    zero = jnp.zeros_like(sin_r)
    cos = jnp.concatenate([cos_r, cos_r, cos_c, cos_c], axis=-1)
    sin_a = jnp.concatenate([-sin_r, zero, -sin_c, zero], axis=-1)
    sin_b = jnp.concatenate([zero, sin_r, zero, sin_c], axis=-1)
    return cos, sin_a, sin_b


def kernel(x, c, w_ada, b_ada, w_in, qk_norm_g, conv_w, conv_b, rg_w_a, rg_b_a, rg_w_i, rg_b_i,
           rg_lam, mix_out_g, w_o, ln_g, ln_b, w_ffn_in, w_down):
    batch, seq, d = x.shape
    depth = w_ada.shape[0]
    m = batch * seq
    alpha = (2 * depth) ** 0.25
    assert d == D_MODEL and seq % ATTN_TK == 0 and seq % INPROJ_TM == 0 and m % MM_TM == 0

    cos, sin_a, sin_b = _rope_tables(seq)
    c_pad = jnp.pad(c, ((0, SUBLANES - batch), (0, 0)))
    mod = _ada(c_pad, w_ada, b_ada)[:, :batch].reshape(depth, batch, 6, 1, d)

    w_in_b = w_in.astype(BF16)
    w_o_b = w_o.astype(BF16)
    pad_ff = D_FF_PAD - D_FF
    w_ffn_b = jnp.pad(w_ffn_in.astype(BF16).reshape(depth, d, 2, D_FF),
                      ((0, 0), (0, 0), (0, 0), (0, pad_ff))).reshape(depth, d, 2 * D_FF_PAD)
    w_down_b = jnp.pad(w_down.astype(BF16), ((0, 0), (0, pad_ff), (0, 0)))

    x2 = x.reshape(m, d)
    u = _modulate(x2, mod[0, :, 1], mod[0, :, 0], seq)
    for l in range(depth):
        sh2, sc2, g1, g2 = mod[l, :, 3], mod[l, :, 4], mod[l, :, 2], mod[l, :, 5]
        q, k, vt, xr, xg = _inproj(u, w_in_b[l], qk_norm_g[l], cos, sin_a, sin_b, seq)
        y_attn = _attention(q, k, vt, mix_out_g[l, :D_ATTN].reshape(1, D_ATTN), batch, seq)
        y_rnn = _rglru(xr, xg, conv_w[l], conv_b[l], rg_w_a[l], rg_b_a[l], rg_w_i[l], rg_b_i[l],
                       rg_lam[l], batch, seq)
        y_rnn = _rms_rows(y_rnn, mix_out_g[l, D_ATTN:].reshape(1, D_RNN))
        y = _outproj(y_attn, y_rnn, w_o_b[l])
        x2, u = _deepnorm(x2, y, g1, ln_g[l, 0], ln_b[l, 0], (sc2, sh2), alpha, seq)
        h = _ffn_in(u, w_ffn_b[l])
        y = _ffn_down(h, w_down_b[l])
        nxt = (mod[l + 1, :, 1], mod[l + 1, :, 0]) if l + 1 < depth else None
        x2, u = _deepnorm(x2, y, g2, ln_g[l, 1], ln_b[l, 1], nxt, alpha, seq)
    return x2.reshape(batch, seq, d)
```

```python
import functools
import math

import jax
import jax.numpy as jnp
from jax import lax
from jax.experimental import pallas as pl
from jax.experimental.pallas import tpu as pltpu

D_MODEL = 4096
GRID_W = 64
HEAD_DIM = 128
D_ATTN = D_MODEL // 2
N_Q_HEADS = D_ATTN // HEAD_DIM
N_KV_HEADS = 4
Q_PER_KV = N_Q_HEADS // N_KV_HEADS
D_KV = N_KV_HEADS * HEAD_DIM
D_RNN = D_MODEL - D_ATTN
RNN_BLOCK = 128
N_RNN_BLOCKS = D_RNN // RNN_BLOCK
D_IN = D_ATTN + 2 * D_KV + 2 * D_RNN
CONV_W = 4
RG_C = 8.0
ROPE_THETA = 10000.0
ROPE_AXIS_DIM = HEAD_DIM // 2
D_FF = (-(-8 * D_MODEL // (3 * 256))) * 256
LN_EPS = 1e-5
RMS_EPS = 1e-6
Q_SCALE = HEAD_DIM ** -0.5 * math.log2(math.e)

V7X_VMEM_BYTES = 64 * 1024 * 1024
VMEM_LIMIT = V7X_VMEM_BYTES - 8 * 1024 * 1024
LANES = 128
SUBLANES = 8

BF16 = jnp.bfloat16
F32 = jnp.float32

ADA_TN = 1024
INPROJ_TM = 512
INPROJ_TN = 1024
ATTN_TQ = 256
ATTN_TK = 1024
MM_TM = 1024
MM_TN = 1024
FFN_TN = 512
DOWN_TM = 512
DOWN_TN = 1024
RNN_SEGMENTS = 32
ROW_TM = 256
RNN_CHUNK = 512


def _params(sem):
    return pltpu.CompilerParams(dimension_semantics=sem, vmem_limit_bytes=VMEM_LIMIT)


def _ada_kernel(c_ref, w_ref, b_ref, o_ref):
    c = c_ref[...]
    c_act = (c * jax.nn.sigmoid(c)).astype(BF16)
    o_ref[...] = jnp.dot(c_act, w_ref[...].astype(BF16), preferred_element_type=F32) + b_ref[...]


def _ada(c_pad, w_ada, b_ada):
    depth, d, n = w_ada.shape
    return pl.pallas_call(
        _ada_kernel,
        out_shape=jax.ShapeDtypeStruct((depth, SUBLANES, n), F32),
        grid=(depth, n // ADA_TN),
        in_specs=[
            pl.BlockSpec((SUBLANES, d), lambda l, j: (0, 0)),
            pl.BlockSpec((None, d, ADA_TN), lambda l, j: (l, 0, j)),
            pl.BlockSpec((None, 1, ADA_TN), lambda l, j: (l, 0, j)),
        ],
        out_specs=pl.BlockSpec((None, SUBLANES, ADA_TN), lambda l, j: (l, 0, j)),
        compiler_params=_params(("arbitrary", "arbitrary")),
        name="ada_mod",
    )(c_pad, w_ada, b_ada.reshape(depth, 1, n))


def _modulate_kernel(x_ref, sc_ref, sh_ref, u_ref):
    u_ref[...] = (x_ref[...] * (1.0 + sc_ref[...]) + sh_ref[...]).astype(BF16)


def _modulate(x2, sc, sh, seq):
    m, d = x2.shape
    per_b = seq // ROW_TM
    vec = pl.BlockSpec((None, 1, d), lambda i: (i // per_b, 0, 0))
    return pl.pallas_call(
        _modulate_kernel,
        out_shape=jax.ShapeDtypeStruct((m, d), BF16),
        grid=(m // ROW_TM,),
        in_specs=[pl.BlockSpec((ROW_TM, d), lambda i: (i, 0)), vec, vec],
        out_specs=pl.BlockSpec((ROW_TM, d), lambda i: (i, 0)),
        compiler_params=_params(("arbitrary",)),
        name="modulate",
    )(x2, sc, sh)


def _rms_rope(xh, gain, cos, sin_a, sin_b):
    r = lax.rsqrt(jnp.mean(xh * xh, axis=-1, keepdims=True) + RMS_EPS)
    xn = xh * r * gain
    return xn * cos + pltpu.roll(xn, 96, 1) * sin_a + pltpu.roll(xn, 32, 1) * sin_b


def _inproj_kernel(u_ref, w_ref, g_ref, cos_ref, sa_ref, sb_ref,
                   q_ref, k_ref, vt_ref, xr_ref, xg_ref):
    j = pl.program_id(1)
    z = jnp.dot(u_ref[...], w_ref[...], preferred_element_type=F32)
    heads_per_tile = INPROJ_TN // HEAD_DIM

    @pl.when(j < 2)
    def _():
        cos, sa, sb = cos_ref[...], sa_ref[...], sb_ref[...]
        gain = g_ref[0:1, :]
        for h in range(heads_per_tile):
            xh = z[:, h * HEAD_DIM:(h + 1) * HEAD_DIM]
            q = _rms_rope(xh, gain, cos, sa, sb) * Q_SCALE
            q_ref[h] = q.astype(BF16)

    @pl.when(j == 2)
    def _():
        cos, sa, sb = cos_ref[...], sa_ref[...], sb_ref[...]
        gain = g_ref[1:2, :]
        for h in range(N_KV_HEADS):
            xh = z[:, h * HEAD_DIM:(h + 1) * HEAD_DIM]
            k_ref[h] = _rms_rope(xh, gain, cos, sa, sb).astype(BF16)
            vh = z[:, D_KV + h * HEAD_DIM:D_KV + (h + 1) * HEAD_DIM]
            vt_ref[h] = vh.T.astype(BF16)

    @pl.when((j == 3) | (j == 4))
    def _():
        xr_ref[...] = z

    @pl.when(j >= 5)
    def _():
        xg_ref[...] = z


def _inproj(u, w_in, qk_g, cos, sin_a, sin_b, seq):
    m, d = u.shape
    tm, tn = INPROJ_TM, INPROJ_TN
    per_b = seq // tm
    tab = pl.BlockSpec((tm, HEAD_DIM), lambda i, j: (i % per_b, 0))
    q_tiles = D_ATTN // tn
    return pl.pallas_call(
        _inproj_kernel,
        out_shape=(
            jax.ShapeDtypeStruct((N_Q_HEADS, m, HEAD_DIM), BF16),
            jax.ShapeDtypeStruct((N_KV_HEADS, m, HEAD_DIM), BF16),
            jax.ShapeDtypeStruct((N_KV_HEADS, HEAD_DIM, m), BF16),
            jax.ShapeDtypeStruct((m, D_RNN), F32),
            jax.ShapeDtypeStruct((m, D_RNN), F32),
        ),
        grid=(m // tm, D_IN // tn),
        in_specs=[
            pl.BlockSpec((tm, d), lambda i, j: (i, 0)),
            pl.BlockSpec((d, tn), lambda i, j: (0, j)),
            pl.BlockSpec((2, HEAD_DIM), lambda i, j: (0, 0)),
            tab, tab, tab,
        ],
        out_specs=(
            pl.BlockSpec((tn // HEAD_DIM, tm, HEAD_DIM),
                         lambda i, j: (jnp.minimum(j, q_tiles - 1), i, 0)),
            pl.BlockSpec((N_KV_HEADS, tm, HEAD_DIM), lambda i, j: (0, i, 0)),
            pl.BlockSpec((N_KV_HEADS, HEAD_DIM, tm), lambda i, j: (0, 0, i)),
            pl.BlockSpec((tm, tn), lambda i, j: (i, jnp.clip(j - 3, 0, 1))),
            pl.BlockSpec((tm, tn), lambda i, j: (i, jnp.clip(j - 5, 0, 1))),
        ),
        compiler_params=_params(("arbitrary", "arbitrary")),
        name="in_proj",
    )(u, w_in, qk_g, cos, sin_a, sin_b)


def _attn_kernel(q_ref, k_ref, vt_ref, g_ref, o_ref, m_s, l_s, acc_s):
    j = pl.program_id(2)
    tq = q_ref.shape[1]

    @pl.when(j == 0)
    def _():
        m_s[...] = jnp.full(m_s.shape, -jnp.inf, F32)
        l_s[...] = jnp.zeros(l_s.shape, F32)
        acc_s[...] = jnp.zeros(acc_s.shape, F32)

    def scores(kh, g):
        return lax.dot_general(k_ref[kh], q_ref[Q_PER_KV * kh + g], (((1,), (1,)), ((), ())),
                               preferred_element_type=F32)

    def update(kh, g, s_t):
        cols = slice(g * tq, (g + 1) * tq)
        m_old = m_s[kh:kh + 1, cols]
        m_new = jnp.maximum(m_old, jnp.max(s_t, axis=0, keepdims=True))
        alpha = jnp.exp2(m_old - m_new)
        p = jnp.exp2(s_t - m_new)
        l_s[kh:kh + 1, cols] = alpha * l_s[kh:kh + 1, cols] + jnp.sum(p, axis=0, keepdims=True)
        pv = jnp.dot(vt_ref[kh], p.astype(BF16), preferred_element_type=F32)
        acc_s[kh, :, cols] = alpha * acc_s[kh, :, cols] + pv
        m_s[kh:kh + 1, cols] = m_new

    blocks = [(kh, g) for kh in range(N_KV_HEADS) for g in range(Q_PER_KV)]
    s_next = scores(*blocks[0])
    for c, (kh, g) in enumerate(blocks):
        s_cur = s_next
        if c + 1 < len(blocks):
            s_next = scores(*blocks[c + 1])
        update(kh, g, s_cur)

    @pl.when(j == pl.num_programs(2) - 1)
    def _():
        outs = []
        ss = jnp.zeros((1, tq), F32)
        for kh in range(N_KV_HEADS):
            o = acc_s[kh] / l_s[kh:kh + 1, :]
            sq = jnp.sum(o * o, axis=0, keepdims=True)
            for g in range(Q_PER_KV):
                ss = ss + sq[:, g * tq:(g + 1) * tq]
            outs.append(o)
        r = lax.rsqrt(ss * (1.0 / D_ATTN) + RMS_EPS)
        for kh in range(N_KV_HEADS):
            for g in range(Q_PER_KV):
                h = kh * Q_PER_KV + g
                oh = (outs[kh][:, g * tq:(g + 1) * tq] * r).T
                o_ref[:, h * HEAD_DIM:(h + 1) * HEAD_DIM] = (
                    oh * g_ref[:, h * HEAD_DIM:(h + 1) * HEAD_DIM]).astype(BF16)


def _attention(q, k, vt, out_g_attn, batch, seq):
    tq, tk = ATTN_TQ, ATTN_TK
    nq, nk = seq // tq, seq // tk
    m = batch * seq
    return pl.pallas_call(
        _attn_kernel,
        out_shape=jax.ShapeDtypeStruct((m, D_ATTN), BF16),
        grid=(batch, nq, nk),
        in_specs=[
            pl.BlockSpec((N_Q_HEADS, tq, HEAD_DIM), lambda b, i, j: (0, b * nq + i, 0)),
            pl.BlockSpec((N_KV_HEADS, tk, HEAD_DIM), lambda b, i, j: (0, b * nk + j, 0)),
            pl.BlockSpec((N_KV_HEADS, HEAD_DIM, tk), lambda b, i, j: (0, 0, b * nk + j)),
            pl.BlockSpec((1, D_ATTN), lambda b, i, j: (0, 0)),
        ],
        out_specs=pl.BlockSpec((tq, D_ATTN), lambda b, i, j: (b * nq + i, 0)),
        scratch_shapes=[
            pltpu.VMEM((SUBLANES, Q_PER_KV * tq), F32),
            pltpu.VMEM((SUBLANES, Q_PER_KV * tq), F32),
            pltpu.VMEM((N_KV_HEADS, HEAD_DIM, Q_PER_KV * tq), F32),
        ],
        compiler_params=_params(("arbitrary", "arbitrary", "arbitrary")),
        name="attention",
    )(q, k, vt, out_g_attn)


def _gelu_tanh(x):
    return 0.5 * x * (1.0 + jnp.tanh(math.sqrt(2.0 / math.pi) * (x + 0.044715 * (x * x * x))))


def _softplus(x):
    return jnp.maximum(x, 0.0) + jnp.log1p(jnp.exp(-jnp.abs(x)))


def _rglru_kernel(xr_ref, xg_ref, cw_ref, cb_ref, wa_ref, ba_ref, wi_ref, bi_ref, lam_ref,
                  y_ref, xe_s, a_s, u_s):
    seq = xr_ref.shape[0]
    ns = RNN_SEGMENTS
    steps = seq // ns
    t = RNN_CHUNK
    n_chunks = seq // t
    chains = ns // SUBLANES

    cw = cw_ref[...]
    cb = cb_ref[...]
    coef = -RG_C * _softplus(-lam_ref[...])
    w_a = [wa_ref[d].astype(BF16) for d in range(2)]
    w_i = [wi_ref[d].astype(BF16) for d in range(2)]

    seg_id = lax.broadcasted_iota(jnp.int32, (ns, LANES), 0)
    xe_s[pl.ds(0, ns), :] = jnp.where(
        seg_id == 0, 0.0, pltpu.roll(xr_ref[pl.ds(seq - ns, ns), :], 1, 0))
    for k in range(2):
        xe_s[pl.ds(ns + seq + k * ns, ns), :] = jnp.where(
            seg_id == ns - 1, 0.0, pltpu.roll(xr_ref[pl.ds(k * ns, ns), :], ns - 1, 0))

    def stage(c, carry):
        t0 = pl.multiple_of(c * t, t)
        xe_s[pl.ds(ns + t0, t), :] = xr_ref[pl.ds(t0, t), :]
        return carry

    lax.fori_loop(0, n_chunks, stage, 0)

    def gates(c, carry):
        t0 = pl.multiple_of(c * t, t)
        xc = cb
        for tap in range(CONV_W):
            xc = xc + cw[tap:tap + 1, :] * xe_s[pl.ds(t0 + tap * ns, t), :]
        xcb = xc.astype(BF16)
        for d in range(2):
            r = jax.nn.sigmoid(jnp.dot(xcb, w_a[d], preferred_element_type=F32) + ba_ref[d:d + 1, :])
            i = jax.nn.sigmoid(jnp.dot(xcb, w_i[d], preferred_element_type=F32) + bi_ref[d:d + 1, :])
            log_a = coef[d:d + 1, :] * r
            a = jnp.exp(log_a)
            a_s[d, pl.ds(t0, t), :] = a
            u_s[d, pl.ds(t0, t), :] = jnp.sqrt(-jnp.tanh(log_a) * (a * a + 1.0)) * i * xc
        return carry

    lax.fori_loop(0, n_chunks, gates, 0)

    def seg_rows(g, j):
        return pl.ds(pl.multiple_of(j * ns + g * SUBLANES, SUBLANES), SUBLANES)

    def local(j, carry):
        out = []
        for d in range(2):
            jj = j if d == 0 else steps - 1 - j
            for g in range(chains):
                h, p = carry[d * chains + g]
                a = a_s[d, seg_rows(g, jj), :]
                out.append((a * h + u_s[d, seg_rows(g, jj), :], a * p))
        return tuple(out)

    zero = jnp.zeros((SUBLANES, LANES), F32)
    one = jnp.ones((SUBLANES, LANES), F32)
    ends = lax.fori_loop(0, steps, local, tuple((zero, one) for _ in range(2 * chains)), unroll=8)

    starts = []
    for d in range(2):
        h_end = jnp.concatenate([ends[d * chains + g][0] for g in range(chains)], axis=0)
        p_end = jnp.concatenate([ends[d * chains + g][1] for g in range(chains)], axis=0)
        order = range(ns) if d == 0 else range(ns - 1, -1, -1)
        carry = jnp.zeros((1, LANES), F32)
        rows = [None] * ns
        for s in order:
            rows[s] = carry
            carry = p_end[s:s + 1, :] * carry + h_end[s:s + 1, :]
        init = jnp.concatenate(rows, axis=0)
        starts.extend(init[g * SUBLANES:(g + 1) * SUBLANES] for g in range(chains))

    def final(j, carry):
        out = []
        for d in range(2):
            jj = j if d == 0 else steps - 1 - j
            for g in range(chains):
                h = a_s[d, seg_rows(g, jj), :] * carry[d * chains + g] + u_s[d, seg_rows(g, jj), :]
                u_s[d, seg_rows(g, jj), :] = h
                out.append(h)
        return tuple(out)

    lax.fori_loop(0, steps, final, tuple(starts), unroll=8)

    def combine(c, carry):
        t0 = pl.multiple_of(c * t, t)
        h = u_s[0, pl.ds(t0, t), :] + u_s[1, pl.ds(t0, t), :]
        y_ref[pl.ds(t0, t), :] = _gelu_tanh(xg_ref[pl.ds(t0, t), :]) * h
        return carry

    lax.fori_loop(0, n_chunks, combine, 0)


def _rglru(xr, xg, conv_w, conv_b, w_a, b_a, w_i, b_i, lam, batch, seq):
    m, d_rnn = xr.shape
    blk = RNN_BLOCK
    assert seq % RNN_CHUNK == 0 and RNN_CHUNK % RNN_SEGMENTS == 0 and RNN_SEGMENTS % SUBLANES == 0
    slab = pl.BlockSpec((seq, blk), lambda b, n: (b, n))
    vec2 = pl.BlockSpec((2, blk), lambda b, n: (0, n))
    wspec = pl.BlockSpec((2, None, blk, blk), lambda b, n: (0, n, 0, 0))
    return pl.pallas_call(
        _rglru_kernel,
        out_shape=jax.ShapeDtypeStruct((m, d_rnn), F32),
        grid=(batch, d_rnn // blk),
        in_specs=[
            slab, slab,
            pl.BlockSpec((CONV_W, blk), lambda b, n: (0, n)),
            pl.BlockSpec((1, blk), lambda b, n: (0, n)),
            wspec, vec2, wspec, vec2, vec2,
        ],
        out_specs=slab,
        scratch_shapes=[pltpu.VMEM((seq + (CONV_W - 1) * RNN_SEGMENTS, blk), F32),
                        pltpu.VMEM((2, seq, blk), F32), pltpu.VMEM((2, seq, blk), F32)],
        compiler_params=_params(("arbitrary", "arbitrary")),
        name="rg_lru",
    )(xr, xg, conv_w, conv_b.reshape(1, d_rnn), w_a, b_a, w_i, b_i, lam)


def _rms_rows_kernel(y_ref, g_ref, o_ref):
    y = y_ref[...]
    r = lax.rsqrt(jnp.mean(y * y, axis=-1, keepdims=True) + RMS_EPS)
    o_ref[...] = (y * r * g_ref[...]).astype(BF16)


def _rms_rows(y, gain):
    m, d = y.shape
    return pl.pallas_call(
        _rms_rows_kernel,
        out_shape=jax.ShapeDtypeStruct((m, d), BF16),
        grid=(m // ROW_TM,),
        in_specs=[pl.BlockSpec((ROW_TM, d), lambda i: (i, 0)), pl.BlockSpec((1, d), lambda i: (0, 0))],
        out_specs=pl.BlockSpec((ROW_TM, d), lambda i: (i, 0)),
        compiler_params=_params(("arbitrary",)),
        name="rnn_rms",
    )(y, gain)


def _outproj_kernel(a1_ref, a2_ref, w_ref, o_ref):
    k1 = a1_ref.shape[1]
    o_ref[...] = (jnp.dot(a1_ref[...], w_ref[0:k1, :], preferred_element_type=F32)
                  + jnp.dot(a2_ref[...], w_ref[k1:, :], preferred_element_type=F32))


def _outproj(a1, a2, w_o):
    m, k1 = a1.shape
    k2 = a2.shape[1]
    n = w_o.shape[1]
    return pl.pallas_call(
        _outproj_kernel,
        out_shape=jax.ShapeDtypeStruct((m, n), F32),
        grid=(m // MM_TM, n // MM_TN),
        in_specs=[
            pl.BlockSpec((MM_TM, k1), lambda i, j: (i, 0)),
            pl.BlockSpec((MM_TM, k2), lambda i, j: (i, 0)),
            pl.BlockSpec((k1 + k2, MM_TN), lambda i, j: (0, j)),
        ],
        out_specs=pl.BlockSpec((MM_TM, MM_TN), lambda i, j: (i, j)),
        compiler_params=_params(("arbitrary", "arbitrary")),
        name="out_proj",
    )(a1, a2, w_o)


def _deepnorm_kernel(x_ref, y_ref, gate_ref, g_ref, b_ref, *rest, alpha, modulate):
    s = alpha * x_ref[...] + gate_ref[...] * y_ref[...]
    mu = jnp.mean(s, axis=-1, keepdims=True)
    sc = s - mu
    var = jnp.mean(sc * sc, axis=-1, keepdims=True)
    out = sc * lax.rsqrt(var + LN_EPS) * g_ref[...] + b_ref[...]
    if modulate:
        sc_ref, sh_ref, o_ref, u_ref = rest
        u_ref[...] = (out * (1.0 + sc_ref[...]) + sh_ref[...]).astype(BF16)
    else:
        (o_ref,) = rest
    o_ref[...] = out


def _deepnorm(x2, y, gate, ln_g, ln_b, nxt, alpha, seq):
    m, d = x2.shape
    per_b = seq // ROW_TM
    rows = pl.BlockSpec((ROW_TM, d), lambda i: (i, 0))
    vec_b = pl.BlockSpec((None, 1, d), lambda i: (i // per_b, 0, 0))
    vec = pl.BlockSpec((1, d), lambda i: (0, 0))
    modulate = nxt is not None
    in_specs = [rows, rows, vec_b, vec, vec]
    args = [x2, y, gate, ln_g.reshape(1, d), ln_b.reshape(1, d)]
    out_shape = [jax.ShapeDtypeStruct((m, d), F32)]
    out_specs = [rows]
    if modulate:
        in_specs += [vec_b, vec_b]
        args += list(nxt)
        out_shape.append(jax.ShapeDtypeStruct((m, d), BF16))
        out_specs.append(rows)
    res = pl.pallas_call(
        functools.partial(_deepnorm_kernel, alpha=alpha, modulate=modulate),
        out_shape=tuple(out_shape),
        grid=(m // ROW_TM,),
        in_specs=in_specs,
        out_specs=tuple(out_specs),
        compiler_params=_params(("arbitrary",)),
        name="deepnorm",
    )(*args)
    return res if modulate else (res[0], None)


def _ffn_in_kernel(u_ref, wg_ref, wu_ref, h_ref):
    u = u_ref[...]
    gate = jnp.dot(u, wg_ref[...], preferred_element_type=F32)
    up = jnp.dot(u, wu_ref[...], preferred_element_type=F32)
    h_ref[...] = (gate * jax.nn.sigmoid(gate) * up).astype(BF16)


def _ffn_in(u, w_gate, w_up):
    m, d = u.shape
    d_ff = w_gate.shape[1]
    return pl.pallas_call(
        _ffn_in_kernel,
        out_shape=jax.ShapeDtypeStruct((m, d_ff), BF16),
        grid=(m // MM_TM, pl.cdiv(d_ff, FFN_TN)),
        in_specs=[
            pl.BlockSpec((MM_TM, d), lambda i, j: (i, 0)),
            pl.BlockSpec((d, FFN_TN), lambda i, j: (0, j)),
            pl.BlockSpec((d, FFN_TN), lambda i, j: (0, j)),
        ],
        out_specs=pl.BlockSpec((MM_TM, FFN_TN), lambda i, j: (i, j)),
        compiler_params=_params(("arbitrary", "arbitrary")),
        name="ffn_in",
    )(u, w_gate, w_up)


def _ffn_down_kernel(h_ref, w_ref, o_ref):
    o_ref[...] = jnp.dot(h_ref[...], w_ref[...], preferred_element_type=F32)


def _ffn_down(h, w_down):
    m, kdim = h.shape
    n = w_down.shape[1]
    return pl.pallas_call(
        _ffn_down_kernel,
        out_shape=jax.ShapeDtypeStruct((m, n), F32),
        grid=(n // DOWN_TN, m // DOWN_TM),
        in_specs=[
            pl.BlockSpec((DOWN_TM, kdim), lambda j, i: (i, 0)),
            pl.BlockSpec((kdim, DOWN_TN), lambda j, i: (0, j), pipeline_mode=pl.Buffered(1)),
        ],
        out_specs=pl.BlockSpec((DOWN_TM, DOWN_TN), lambda j, i: (i, j)),
        compiler_params=_params(("arbitrary", "arbitrary")),
        name="ffn_down",
    )(h, w_down)


def _rope_tables(seq):
    pos = jnp.arange(seq, dtype=jnp.int32)
    row = (pos // GRID_W).astype(F32)
    col = (pos % GRID_W).astype(F32)
    inv_freq = jnp.power(ROPE_THETA, -jnp.arange(0, ROPE_AXIS_DIM, 2, dtype=F32) / ROPE_AXIS_DIM)
    ang_r = row[:, None] * inv_freq[None, :]
    ang_c = col[:, None] * inv_freq[None, :]
    cos_r, sin_r, cos_c, sin_c = jnp.cos(ang_r), jnp.sin(ang_r), jnp.cos(ang_c), jnp.sin(ang_c)
    zero = jnp.zeros_like(sin_r)
    cos = jnp.concatenate([cos_r, cos_r, cos_c, cos_c], axis=-1)
    sin_a = jnp.concatenate([-sin_r, zero, -sin_c, zero], axis=-1)
    sin_b = jnp.concatenate([zero, sin_r, zero, sin_c], axis=-1)
    return cos, sin_a, sin_b


def kernel(x, c, w_ada, b_ada, w_in, qk_norm_g, conv_w, conv_b, rg_w_a, rg_b_a, rg_w_i, rg_b_i,
           rg_lam, mix_out_g, w_o, ln_g, ln_b, w_ffn_in, w_down):
    batch, seq, d = x.shape
    depth = w_ada.shape[0]
    m = batch * seq
    alpha = (2 * depth) ** 0.25
    assert d == D_MODEL and seq % ATTN_TK == 0 and seq % INPROJ_TM == 0 and m % MM_TM == 0
    assert m % DOWN_TM == 0 and d % DOWN_TN == 0

    ns = RNN_SEGMENTS

    def interleave(a):
        lead = a.shape[:-2]
        return a.reshape(*lead, ns, seq // ns, a.shape[-1]).swapaxes(-3, -2).reshape(a.shape)

    cos, sin_a, sin_b = (interleave(tab) for tab in _rope_tables(seq))
    c_pad = jnp.pad(c, ((0, SUBLANES - batch), (0, 0)))
    mod = _ada(c_pad, w_ada, b_ada)[:, :batch].reshape(depth, batch, 6, 1, d)

    w_in_b = w_in.astype(BF16)
    w_o_b = w_o.astype(BF16)
    w_gate_b = w_ffn_in[:, :, :D_FF].astype(BF16)
    w_up_b = w_ffn_in[:, :, D_FF:].astype(BF16)
    w_down_b = w_down.astype(BF16)

    x2 = interleave(x).reshape(m, d)
    u = _modulate(x2, mod[0, :, 1], mod[0, :, 0], seq)
    for l in range(depth):
        sh2, sc2, g1, g2 = mod[l, :, 3], mod[l, :, 4], mod[l, :, 2], mod[l, :, 5]
        q, k, vt, xr, xg = _inproj(u, w_in_b[l], qk_norm_g[l], cos, sin_a, sin_b, seq)
        y_attn = _attention(q, k, vt, mix_out_g[l, :D_ATTN].reshape(1, D_ATTN), batch, seq)
        y_rnn = _rglru(xr, xg, conv_w[l], conv_b[l], rg_w_a[l], rg_b_a[l], rg_w_i[l], rg_b_i[l],
                       rg_lam[l], batch, seq)
        y_rnn = _rms_rows(y_rnn, mix_out_g[l, D_ATTN:].reshape(1, D_RNN))
        y = _outproj(y_attn, y_rnn, w_o_b[l])
        x2, u = _deepnorm(x2, y, g1, ln_g[l, 0], ln_b[l, 0], (sc2, sh2), alpha, seq)
        h = _ffn_in(u, w_gate_b[l], w_up_b[l])
        y = _ffn_down(h, w_down_b[l])
        nxt = (mod[l + 1, :, 1], mod[l + 1, :, 0]) if l + 1 < depth else None
        x2, u = _deepnorm(x2, y, g2, ln_g[l, 1], ln_b[l, 1], nxt, alpha, seq)
    out = x2.reshape(batch, seq // ns, ns, d).swapaxes(1, 2)
    return out.reshape(batch, seq, d)
```

```python
import functools
import math

import jax
import jax.numpy as jnp
from jax import lax
from jax.experimental import pallas as pl
from jax.experimental.pallas import tpu as pltpu

D_MODEL = 4096
GRID_W = 64
HEAD_DIM = 128
D_ATTN = D_MODEL // 2
N_Q_HEADS = D_ATTN // HEAD_DIM
N_KV_HEADS = 4
Q_PER_KV = N_Q_HEADS // N_KV_HEADS
D_KV = N_KV_HEADS * HEAD_DIM
D_RNN = D_MODEL - D_ATTN
RNN_BLOCK = 128
N_RNN_BLOCKS = D_RNN // RNN_BLOCK
D_IN = D_ATTN + 2 * D_KV + 2 * D_RNN
CONV_W = 4
RG_C = 8.0
ROPE_THETA = 10000.0
ROPE_AXIS_DIM = HEAD_DIM // 2
D_FF = (-(-8 * D_MODEL // (3 * 256))) * 256
LN_EPS = 1e-5
RMS_EPS = 1e-6
Q_SCALE = HEAD_DIM ** -0.5 * math.log2(math.e)

V7X_VMEM_BYTES = 64 * 1024 * 1024
VMEM_LIMIT = V7X_VMEM_BYTES - 8 * 1024 * 1024
LANES = 128
SUBLANES = 8

BF16 = jnp.bfloat16
F32 = jnp.float32

ADA_TN = 1024
INPROJ_TM = 512
INPROJ_TN = 1024
ATTN_TQ = 512
ATTN_TK = 1024
ATTN_SCORE_SPLIT = 4
ATTN_LOOKAHEAD = 2
VT_ROWS = HEAD_DIM + 16
MM_TM = 1024
MM_TN = 1024
FFN_TN = 512
DOWN_TM = 512
DOWN_TN = 1024
RNN_SEGMENTS = 32
ROW_TM = 256
RNN_CHUNK = 512


def _params(sem):
    return pltpu.CompilerParams(dimension_semantics=sem, vmem_limit_bytes=VMEM_LIMIT)


def _ada_kernel(c_ref, w_ref, b_ref, o_ref):
    c = c_ref[...]
    c_act = (c * jax.nn.sigmoid(c)).astype(BF16)
    o_ref[...] = jnp.dot(c_act, w_ref[...].astype(BF16), preferred_element_type=F32) + b_ref[...]


def _ada(c_pad, w_ada, b_ada):
    depth, d, n = w_ada.shape
    return pl.pallas_call(
        _ada_kernel,
        out_shape=jax.ShapeDtypeStruct((depth, SUBLANES, n), F32),
        grid=(depth, n // ADA_TN),
        in_specs=[
            pl.BlockSpec((SUBLANES, d), lambda l, j: (0, 0)),
            pl.BlockSpec((None, d, ADA_TN), lambda l, j: (l, 0, j)),
            pl.BlockSpec((None, 1, ADA_TN), lambda l, j: (l, 0, j)),
        ],
        out_specs=pl.BlockSpec((None, SUBLANES, ADA_TN), lambda l, j: (l, 0, j)),
        compiler_params=_params(("arbitrary", "arbitrary")),
        name="ada_mod",
    )(c_pad, w_ada, b_ada.reshape(depth, 1, n))


def _modulate_kernel(x_ref, sc_ref, sh_ref, u_ref):
    u_ref[...] = (x_ref[...] * (1.0 + sc_ref[...]) + sh_ref[...]).astype(BF16)


def _modulate(x2, sc, sh, seq):
    m, d = x2.shape
    per_b = seq // ROW_TM
    vec = pl.BlockSpec((None, 1, d), lambda i: (i // per_b, 0, 0))
    return pl.pallas_call(
        _modulate_kernel,
        out_shape=jax.ShapeDtypeStruct((m, d), BF16),
        grid=(m // ROW_TM,),
        in_specs=[pl.BlockSpec((ROW_TM, d), lambda i: (i, 0)), vec, vec],
        out_specs=pl.BlockSpec((ROW_TM, d), lambda i: (i, 0)),
        compiler_params=_params(("arbitrary",)),
        name="modulate",
    )(x2, sc, sh)


def _rms_rope(xh, gain, cos, sin_a, sin_b):
    r = lax.rsqrt(jnp.mean(xh * xh, axis=-1, keepdims=True) + RMS_EPS)
    xn = xh * r * gain
    return xn * cos + pltpu.roll(xn, 96, 1) * sin_a + pltpu.roll(xn, 32, 1) * sin_b


def _inproj_kernel(u_ref, w_ref, g_ref, cos_ref, sa_ref, sb_ref,
                   q_ref, k_ref, vt_ref, xr_ref, xg_ref):
    j = pl.program_id(1)
    z = jnp.dot(u_ref[...], w_ref[...], preferred_element_type=F32)
    heads_per_tile = INPROJ_TN // HEAD_DIM

    @pl.when(j < 2)
    def _():
        cos, sa, sb = cos_ref[...], sa_ref[...], sb_ref[...]
        gain = g_ref[0:1, :]
        for h in range(heads_per_tile):
            xh = z[:, h * HEAD_DIM:(h + 1) * HEAD_DIM]
            q = _rms_rope(xh, gain, cos, sa, sb) * Q_SCALE
            q_ref[h] = q.astype(BF16)

    @pl.when(j == 2)
    def _():
        cos, sa, sb = cos_ref[...], sa_ref[...], sb_ref[...]
        gain = g_ref[1:2, :]
        for h in range(N_KV_HEADS):
            xh = z[:, h * HEAD_DIM:(h + 1) * HEAD_DIM]
            k_ref[h] = _rms_rope(xh, gain, cos, sa, sb).astype(BF16)
            vh = z[:, D_KV + h * HEAD_DIM:D_KV + (h + 1) * HEAD_DIM]
            vt_ref[h, 0:HEAD_DIM, :] = vh.T.astype(BF16)
            vt_ref[h, HEAD_DIM:, :] = jnp.ones((VT_ROWS - HEAD_DIM, vh.shape[0]), BF16)

    @pl.when((j == 3) | (j == 4))
    def _():
        xr_ref[...] = z

    @pl.when(j >= 5)
    def _():
        xg_ref[...] = z


def _inproj(u, w_in, qk_g, cos, sin_a, sin_b, seq):
    m, d = u.shape
    tm, tn = INPROJ_TM, INPROJ_TN
    per_b = seq // tm
    tab = pl.BlockSpec((tm, HEAD_DIM), lambda i, j: (i % per_b, 0))
    q_tiles = D_ATTN // tn
    return pl.pallas_call(
        _inproj_kernel,
        out_shape=(
            jax.ShapeDtypeStruct((N_Q_HEADS, m, HEAD_DIM), BF16),
            jax.ShapeDtypeStruct((N_KV_HEADS, m, HEAD_DIM), BF16),
            jax.ShapeDtypeStruct((N_KV_HEADS, VT_ROWS, m), BF16),
            jax.ShapeDtypeStruct((m, D_RNN), F32),
            jax.ShapeDtypeStruct((m, D_RNN), F32),
        ),
        grid=(m // tm, D_IN // tn),
        in_specs=[
            pl.BlockSpec((tm, d), lambda i, j: (i, 0)),
            pl.BlockSpec((d, tn), lambda i, j: (0, j)),
            pl.BlockSpec((2, HEAD_DIM), lambda i, j: (0, 0)),
            tab, tab, tab,
        ],
        out_specs=(
            pl.BlockSpec((tn // HEAD_DIM, tm, HEAD_DIM),
                         lambda i, j: (jnp.minimum(j, q_tiles - 1), i, 0)),
            pl.BlockSpec((N_KV_HEADS, tm, HEAD_DIM), lambda i, j: (0, i, 0)),
            pl.BlockSpec((N_KV_HEADS, VT_ROWS, tm), lambda i, j: (0, 0, i)),
            pl.BlockSpec((tm, tn), lambda i, j: (i, jnp.clip(j - 3, 0, 1))),
            pl.BlockSpec((tm, tn), lambda i, j: (i, jnp.clip(j - 5, 0, 1))),
        ),
        compiler_params=_params(("arbitrary", "arbitrary")),
        name="in_proj",
    )(u, w_in, qk_g, cos, sin_a, sin_b)


def _attn_kernel(q_ref, k_ref, vt_ref, g_ref, o_ref, m_s, acc_s):
    j = pl.program_id(2)
    tq = q_ref.shape[1]
    tk = k_ref.shape[1]
    rows = tk // ATTN_SCORE_SPLIT

    @pl.when(j == 0)
    def _():
        m_s[...] = jnp.full(m_s.shape, -jnp.inf, F32)
        acc_s[...] = jnp.zeros(acc_s.shape, F32)

    def scores(kh, g):
        q = q_ref[Q_PER_KV * kh + g]
        return [lax.dot_general(k_ref[kh, r * rows:(r + 1) * rows, :], q, (((1,), (1,)), ((), ())),
                                preferred_element_type=F32) for r in range(ATTN_SCORE_SPLIT)]

    def update(kh, g, s_parts):
        cols = slice(g * tq, (g + 1) * tq)
        m_old = m_s[kh:kh + 1, cols]
        m_new = m_old
        for s_t in s_parts:
            m_new = jnp.maximum(m_new, jnp.max(s_t, axis=0, keepdims=True))
        alpha = jnp.exp2(m_old - m_new)
        p = jnp.concatenate([jnp.exp2(s_t - m_new).astype(BF16) for s_t in s_parts], axis=0)
        pv = jnp.dot(vt_ref[kh], p, preferred_element_type=F32)
        acc_s[kh, :, cols] = alpha * acc_s[kh, :, cols] + pv
        m_s[kh:kh + 1, cols] = m_new

    blocks = [(kh, g) for kh in range(N_KV_HEADS) for g in range(Q_PER_KV)]
    pending = [scores(*blk) for blk in blocks[:ATTN_LOOKAHEAD]]
    for c, (kh, g) in enumerate(blocks):
        if c + ATTN_LOOKAHEAD < len(blocks):
            pending.append(scores(*blocks[c + ATTN_LOOKAHEAD]))
        update(kh, g, pending.pop(0))

    @pl.when(j == pl.num_programs(2) - 1)
    def _():
        outs = []
        ss = jnp.zeros((1, tq), F32)
        for kh in range(N_KV_HEADS):
            o = acc_s[kh, 0:HEAD_DIM, :] / acc_s[kh, HEAD_DIM:HEAD_DIM + 1, :]
            sq = jnp.sum(o * o, axis=0, keepdims=True)
            for g in range(Q_PER_KV):
                ss = ss + sq[:, g * tq:(g + 1) * tq]
            outs.append(o)
        r = lax.rsqrt(ss * (1.0 / D_ATTN) + RMS_EPS)
        for kh in range(N_KV_HEADS):
            for g in range(Q_PER_KV):
                h = kh * Q_PER_KV + g
                oh = (outs[kh][:, g * tq:(g + 1) * tq] * r).T
                o_ref[:, h * HEAD_DIM:(h + 1) * HEAD_DIM] = (
                    oh * g_ref[:, h * HEAD_DIM:(h + 1) * HEAD_DIM]).astype(BF16)


def _attention(q, k, vt, out_g_attn, batch, seq):
    tq, tk = ATTN_TQ, ATTN_TK
    nq, nk = seq // tq, seq // tk
    m = batch * seq
    return pl.pallas_call(
        _attn_kernel,
        out_shape=jax.ShapeDtypeStruct((m, D_ATTN), BF16),
        grid=(batch, nq, nk),
        in_specs=[
            pl.BlockSpec((N_Q_HEADS, tq, HEAD_DIM), lambda b, i, j: (0, b * nq + i, 0)),
            pl.BlockSpec((N_KV_HEADS, tk, HEAD_DIM), lambda b, i, j: (0, b * nk + j, 0)),
            pl.BlockSpec((N_KV_HEADS, VT_ROWS, tk), lambda b, i, j: (0, 0, b * nk + j)),
            pl.BlockSpec((1, D_ATTN), lambda b, i, j: (0, 0)),
        ],
        out_specs=pl.BlockSpec((tq, D_ATTN), lambda b, i, j: (b * nq + i, 0)),
        scratch_shapes=[
            pltpu.VMEM((SUBLANES, Q_PER_KV * tq), F32),
            pltpu.VMEM((N_KV_HEADS, VT_ROWS, Q_PER_KV * tq), F32),
        ],
        compiler_params=_params(("arbitrary", "arbitrary", "arbitrary")),
        name="attention",
    )(q, k, vt, out_g_attn)


def _gelu_tanh(x):
    return 0.5 * x * (1.0 + jnp.tanh(math.sqrt(2.0 / math.pi) * (x + 0.044715 * (x * x * x))))


def _softplus(x):
    return jnp.maximum(x, 0.0) + jnp.log1p(jnp.exp(-jnp.abs(x)))


def _rglru_kernel(xr_ref, xg_ref, cw_ref, cb_ref, wa_ref, ba_ref, wi_ref, bi_ref, lam_ref,
                  y_ref, xe_s, a_s, u_s):
    seq = xr_ref.shape[0]
    ns = RNN_SEGMENTS
    steps = seq // ns
    t = RNN_CHUNK
    n_chunks = seq // t
    chains = ns // SUBLANES

    cw = cw_ref[...]
    cb = cb_ref[...]
    coef = -RG_C * _softplus(-lam_ref[...])
    w_a = [wa_ref[d].astype(BF16) for d in range(2)]
    w_i = [wi_ref[d].astype(BF16) for d in range(2)]

    seg_id = lax.broadcasted_iota(jnp.int32, (ns, LANES), 0)
    xe_s[pl.ds(0, ns), :] = jnp.where(
        seg_id == 0, 0.0, pltpu.roll(xr_ref[pl.ds(seq - ns, ns), :], 1, 0))
    for k in range(2):
        xe_s[pl.ds(ns + seq + k * ns, ns), :] = jnp.where(
            seg_id == ns - 1, 0.0, pltpu.roll(xr_ref[pl.ds(k * ns, ns), :], ns - 1, 0))

    def stage(c, carry):
        t0 = pl.multiple_of(c * t, t)
        xe_s[pl.ds(ns + t0, t), :] = xr_ref[pl.ds(t0, t), :]
        return carry

    lax.fori_loop(0, n_chunks, stage, 0)

    def gates(c, carry):
        t0 = pl.multiple_of(c * t, t)
        xc = cb
        for tap in range(CONV_W):
            xc = xc + cw[tap:tap + 1, :] * xe_s[pl.ds(t0 + tap * ns, t), :]
        xcb = xc.astype(BF16)
        for d in range(2):
            r = jax.nn.sigmoid(jnp.dot(xcb, w_a[d], preferred_element_type=F32) + ba_ref[d:d + 1, :])
            i = jax.nn.sigmoid(jnp.dot(xcb, w_i[d], preferred_element_type=F32) + bi_ref[d:d + 1, :])
            log_a = coef[d:d + 1, :] * r
            a = jnp.exp(log_a)
            a_s[d, pl.ds(t0, t), :] = a
            u_s[d, pl.ds(t0, t), :] = jnp.sqrt(-jnp.tanh(log_a) * (a * a + 1.0)) * i * xc
        return carry

    lax.fori_loop(0, n_chunks, gates, 0)

    def seg_rows(g, j):
        return pl.ds(pl.multiple_of(j * ns + g * SUBLANES, SUBLANES), SUBLANES)

    def local(j, carry):
        out = []
        for d in range(2):
            jj = j if d == 0 else steps - 1 - j
            for g in range(chains):
                h, p = carry[d * chains + g]
                a = a_s[d, seg_rows(g, jj), :]
                out.append((a * h + u_s[d, seg_rows(g, jj), :], a * p))
        return tuple(out)

    zero = jnp.zeros((SUBLANES, LANES), F32)
    one = jnp.ones((SUBLANES, LANES), F32)
    ends = lax.fori_loop(0, steps, local, tuple((zero, one) for _ in range(2 * chains)), unroll=8)

    starts = []
    for d in range(2):
        h_end = jnp.concatenate([ends[d * chains + g][0] for g in range(chains)], axis=0)
        p_end = jnp.concatenate([ends[d * chains + g][1] for g in range(chains)], axis=0)
        order = range(ns) if d == 0 else range(ns - 1, -1, -1)
        carry = jnp.zeros((1, LANES), F32)
        rows = [None] * ns
        for s in order:
            rows[s] = carry
            carry = p_end[s:s + 1, :] * carry + h_end[s:s + 1, :]
        init = jnp.concatenate(rows, axis=0)
        starts.extend(init[g * SUBLANES:(g + 1) * SUBLANES] for g in range(chains))

    def final(j, carry):
        out = []
        for d in range(2):
            jj = j if d == 0 else steps - 1 - j
            for g in range(chains):
                h = a_s[d, seg_rows(g, jj), :] * carry[d * chains + g] + u_s[d, seg_rows(g, jj), :]
                u_s[d, seg_rows(g, jj), :] = h
                out.append(h)
        return tuple(out)

    lax.fori_loop(0, steps, final, tuple(starts), unroll=8)

    def combine(c, carry):
        t0 = pl.multiple_of(c * t, t)
        h = u_s[0, pl.ds(t0, t), :] + u_s[1, pl.ds(t0, t), :]
        y_ref[pl.ds(t0, t), :] = _gelu_tanh(xg_ref[pl.ds(t0, t), :]) * h
        return carry

    lax.fori_loop(0, n_chunks, combine, 0)


def _rglru(xr, xg, conv_w, conv_b, w_a, b_a, w_i, b_i, lam, batch, seq):
    m, d_rnn = xr.shape
    blk = RNN_BLOCK
    assert seq % RNN_CHUNK == 0 and RNN_CHUNK % RNN_SEGMENTS == 0 and RNN_SEGMENTS % SUBLANES == 0
    slab = pl.BlockSpec((seq, blk), lambda b, n: (b, n))
    vec2 = pl.BlockSpec((2, blk), lambda b, n: (0, n))
    wspec = pl.BlockSpec((2, None, blk, blk), lambda b, n: (0, n, 0, 0))
    return pl.pallas_call(
        _rglru_kernel,
        out_shape=jax.ShapeDtypeStruct((m, d_rnn), F32),
        grid=(batch, d_rnn // blk),
        in_specs=[
            slab, slab,
            pl.BlockSpec((CONV_W, blk), lambda b, n: (0, n)),
            pl.BlockSpec((1, blk), lambda b, n: (0, n)),
            wspec, vec2, wspec, vec2, vec2,
        ],
        out_specs=slab,
        scratch_shapes=[pltpu.VMEM((seq + (CONV_W - 1) * RNN_SEGMENTS, blk), F32),
                        pltpu.VMEM((2, seq, blk), F32), pltpu.VMEM((2, seq, blk), F32)],
        compiler_params=_params(("arbitrary", "arbitrary")),
        name="rg_lru",
    )(xr, xg, conv_w, conv_b.reshape(1, d_rnn), w_a, b_a, w_i, b_i, lam)


def _rms_rows_kernel(y_ref, g_ref, o_ref):
    y = y_ref[...]
    r = lax.rsqrt(jnp.mean(y * y, axis=-1, keepdims=True) + RMS_EPS)
    o_ref[...] = (y * r * g_ref[...]).astype(BF16)


def _rms_rows(y, gain):
    m, d = y.shape
    return pl.pallas_call(
        _rms_rows_kernel,
        out_shape=jax.ShapeDtypeStruct((m, d), BF16),
        grid=(m // ROW_TM,),
        in_specs=[pl.BlockSpec((ROW_TM, d), lambda i: (i, 0)), pl.BlockSpec((1, d), lambda i: (0, 0))],
        out_specs=pl.BlockSpec((ROW_TM, d), lambda i: (i, 0)),
        compiler_params=_params(("arbitrary",)),
        name="rnn_rms",
    )(y, gain)


def _outproj_kernel(a1_ref, a2_ref, w_ref, o_ref):
    k1 = a1_ref.shape[1]
    o_ref[...] = (jnp.dot(a1_ref[...], w_ref[0:k1, :], preferred_element_type=F32)
                  + jnp.dot(a2_ref[...], w_ref[k1:, :], preferred_element_type=F32)).astype(o_ref.dtype)


def _outproj(a1, a2, w_o):
    m, k1 = a1.shape
    k2 = a2.shape[1]
    n = w_o.shape[1]
    return pl.pallas_call(
        _outproj_kernel,
        out_shape=jax.ShapeDtypeStruct((m, n), BF16),
        grid=(m // MM_TM, n // MM_TN),
        in_specs=[
            pl.BlockSpec((MM_TM, k1), lambda i, j: (i, 0)),
            pl.BlockSpec((MM_TM, k2), lambda i, j: (i, 0)),
            pl.BlockSpec((k1 + k2, MM_TN), lambda i, j: (0, j)),
        ],
        out_specs=pl.BlockSpec((MM_TM, MM_TN), lambda i, j: (i, j)),
        compiler_params=_params(("arbitrary", "arbitrary")),
        name="out_proj",
    )(a1, a2, w_o)


def _deepnorm_kernel(x_ref, y_ref, gate_ref, g_ref, b_ref, *rest, alpha, modulate):
    s = alpha * x_ref[...] + gate_ref[...] * y_ref[...].astype(F32)
    mu = jnp.mean(s, axis=-1, keepdims=True)
    sc = s - mu
    var = jnp.mean(sc * sc, axis=-1, keepdims=True)
    out = sc * lax.rsqrt(var + LN_EPS) * g_ref[...] + b_ref[...]
    if modulate:
        sc_ref, sh_ref, o_ref, u_ref = rest
        u_ref[...] = (out * (1.0 + sc_ref[...]) + sh_ref[...]).astype(BF16)
    else:
        (o_ref,) = rest
    o_ref[...] = out


def _deepnorm(x2, y, gate, ln_g, ln_b, nxt, alpha, seq):
    m, d = x2.shape
    per_b = seq // ROW_TM
    rows = pl.BlockSpec((ROW_TM, d), lambda i: (i, 0))
    vec_b = pl.BlockSpec((None, 1, d), lambda i: (i // per_b, 0, 0))
    vec = pl.BlockSpec((1, d), lambda i: (0, 0))
    modulate = nxt is not None
    in_specs = [rows, rows, vec_b, vec, vec]
    args = [x2, y, gate, ln_g.reshape(1, d), ln_b.reshape(1, d)]
    out_shape = [jax.ShapeDtypeStruct((m, d), F32)]
    out_specs = [rows]
    if modulate:
        in_specs += [vec_b, vec_b]
        args += list(nxt)
        out_shape.append(jax.ShapeDtypeStruct((m, d), BF16))
        out_specs.append(rows)
    res = pl.pallas_call(
        functools.partial(_deepnorm_kernel, alpha=alpha, modulate=modulate),
        out_shape=tuple(out_shape),
        grid=(m // ROW_TM,),
        in_specs=in_specs,
        out_specs=tuple(out_specs),
        compiler_params=_params(("arbitrary",)),
        name="deepnorm",
    )(*args)
    return res if modulate else (res[0], None)


def _ffn_in_kernel(u_ref, wg_ref, wu_ref, h_ref):
    u = u_ref[...]
    gate = jnp.dot(u, wg_ref[...], preferred_element_type=F32)
    up = jnp.dot(u, wu_ref[...], preferred_element_type=F32)
    h_ref[...] = (gate * jax.nn.sigmoid(gate) * up).astype(BF16)


def _ffn_in(u, w_gate, w_up):
    m, d = u.shape
    d_ff = w_gate.shape[1]
    return pl.pallas_call(
        _ffn_in_kernel,
        out_shape=jax.ShapeDtypeStruct((m, d_ff), BF16),
        grid=(m // MM_TM, pl.cdiv(d_ff, FFN_TN)),
        in_specs=[
            pl.BlockSpec((MM_TM, d), lambda i, j: (i, 0)),
            pl.BlockSpec((d, FFN_TN), lambda i, j: (0, j)),
            pl.BlockSpec((d, FFN_TN), lambda i, j: (0, j)),
        ],
        out_specs=pl.BlockSpec((MM_TM, FFN_TN), lambda i, j: (i, j)),
        compiler_params=_params(("arbitrary", "arbitrary")),
        name="ffn_in",
    )(u, w_gate, w_up)


def _ffn_down_kernel(h_ref, w_ref, o_ref):
    o_ref[...] = jnp.dot(h_ref[...], w_ref[...], preferred_element_type=F32).astype(o_ref.dtype)


def _ffn_down(h, w_down):
    m, kdim = h.shape
    n = w_down.shape[1]
    return pl.pallas_call(
        _ffn_down_kernel,
        out_shape=jax.ShapeDtypeStruct((m, n), BF16),
        grid=(n // DOWN_TN, m // DOWN_TM),
        in_specs=[
            pl.BlockSpec((DOWN_TM, kdim), lambda j, i: (i, 0)),
            pl.BlockSpec((kdim, DOWN_TN), lambda j, i: (0, j), pipeline_mode=pl.Buffered(1)),
        ],
        out_specs=pl.BlockSpec((DOWN_TM, DOWN_TN), lambda j, i: (i, j)),
        compiler_params=_params(("arbitrary", "arbitrary")),
        name="ffn_down",
    )(h, w_down)


def _rope_tables(seq):
    pos = jnp.arange(seq, dtype=jnp.int32)
    row = (pos // GRID_W).astype(F32)
    col = (pos % GRID_W).astype(F32)
    inv_freq = jnp.power(ROPE_THETA, -jnp.arange(0, ROPE_AXIS_DIM, 2, dtype=F32) / ROPE_AXIS_DIM)
    ang_r = row[:, None] * inv_freq[None, :]
    ang_c = col[:, None] * inv_freq[None, :]
    cos_r, sin_r, cos_c, sin_c = jnp.cos(ang_r), jnp.sin(ang_r), jnp.cos(ang_c), jnp.sin(ang_c)
    zero = jnp.zeros_like(sin_r)
    cos = jnp.concatenate([cos_r, cos_r, cos_c, cos_c], axis=-1)
    sin_a = jnp.concatenate([-sin_r, zero, -sin_c, zero], axis=-1)
    sin_b = jnp.concatenate([zero, sin_r, zero, sin_c], axis=-1)
    return cos, sin_a, sin_b


def kernel(x, c, w_ada, b_ada, w_in, qk_norm_g, conv_w, conv_b, rg_w_a, rg_b_a, rg_w_i, rg_b_i,
           rg_lam, mix_out_g, w_o, ln_g, ln_b, w_ffn_in, w_down):
    batch, seq, d = x.shape
    depth = w_ada.shape[0]
    m = batch * seq
    alpha = (2 * depth) ** 0.25
    assert d == D_MODEL and seq % ATTN_TK == 0 and seq % INPROJ_TM == 0 and m % MM_TM == 0
    assert m % DOWN_TM == 0 and d % DOWN_TN == 0

    ns = RNN_SEGMENTS

    def interleave(a):
        lead = a.shape[:-2]
        return a.reshape(*lead, ns, seq // ns, a.shape[-1]).swapaxes(-3, -2).reshape(a.shape)

    cos, sin_a, sin_b = (interleave(tab) for tab in _rope_tables(seq))
    c_pad = jnp.pad(c, ((0, SUBLANES - batch), (0, 0)))
    mod = _ada(c_pad, w_ada, b_ada)[:, :batch].reshape(depth, batch, 6, 1, d)

    w_in_b = w_in.astype(BF16)
    w_o_b = w_o.astype(BF16)
    w_gate_b = w_ffn_in[:, :, :D_FF].astype(BF16)
    w_up_b = w_ffn_in[:, :, D_FF:].astype(BF16)
    w_down_b = w_down.astype(BF16)

    x2 = interleave(x).reshape(m, d)
    u = _modulate(x2, mod[0, :, 1], mod[0, :, 0], seq)
    for l in range(depth):
        sh2, sc2, g1, g2 = mod[l, :, 3], mod[l, :, 4], mod[l, :, 2], mod[l, :, 5]
        q, k, vt, xr, xg = _inproj(u, w_in_b[l], qk_norm_g[l], cos, sin_a, sin_b, seq)
        y_attn = _attention(q, k, vt, mix_out_g[l, :D_ATTN].reshape(1, D_ATTN), batch, seq)
        y_rnn = _rglru(xr, xg, conv_w[l], conv_b[l], rg_w_a[l], rg_b_a[l], rg_w_i[l], rg_b_i[l],
                       rg_lam[l], batch, seq)
        y_rnn = _rms_rows(y_rnn, mix_out_g[l, D_ATTN:].reshape(1, D_RNN))
        y = _outproj(y_attn, y_rnn, w_o_b[l])
        x2, u = _deepnorm(x2, y, g1, ln_g[l, 0], ln_b[l, 0], (sc2, sh2), alpha, seq)
        h = _ffn_in(u, w_gate_b[l], w_up_b[l])
        y = _ffn_down(h, w_down_b[l])
        nxt = (mod[l + 1, :, 1], mod[l + 1, :, 0]) if l + 1 < depth else None
        x2, u = _deepnorm(x2, y, g2, ln_g[l, 1], ln_b[l, 1], nxt, alpha, seq)
    out = x2.reshape(batch, seq // ns, ns, d).swapaxes(1, 2)
    return out.reshape(batch, seq, d)
```

```python
import functools
import math

import jax
import jax.numpy as jnp
from jax import lax
from jax.experimental import pallas as pl
from jax.experimental.pallas import tpu as pltpu

D_MODEL = 4096
GRID_W = 64
HEAD_DIM = 128
D_ATTN = D_MODEL // 2
N_Q_HEADS = D_ATTN // HEAD_DIM
N_KV_HEADS = 4
Q_PER_KV = N_Q_HEADS // N_KV_HEADS
D_KV = N_KV_HEADS * HEAD_DIM
D_RNN = D_MODEL - D_ATTN
RNN_BLOCK = 128
N_RNN_BLOCKS = D_RNN // RNN_BLOCK
D_IN = D_ATTN + 2 * D_KV + 2 * D_RNN
CONV_W = 4
RG_C = 8.0
ROPE_THETA = 10000.0
ROPE_AXIS_DIM = HEAD_DIM // 2
D_FF = (-(-8 * D_MODEL // (3 * 256))) * 256
LN_EPS = 1e-5
RMS_EPS = 1e-6
Q_SCALE = HEAD_DIM ** -0.5 * math.log2(math.e)

V7X_VMEM_BYTES = 64 * 1024 * 1024
VMEM_LIMIT = V7X_VMEM_BYTES - 8 * 1024 * 1024
LANES = 128
SUBLANES = 8

BF16 = jnp.bfloat16
F32 = jnp.float32

ADA_TN = 1024
ADA_K_SPLIT = 4
INPROJ_TM = 1024
INPROJ_TN = 512
ATTN_TQ = 512
ATTN_TK = 1024
ATTN_SCORE_SPLIT = 4
ATTN_LOOKAHEAD = 2
VT_ROWS = HEAD_DIM + 16
MM_TM = 1024
MM_TN = 1024
FFN_TN = 512
DOWN_TM = 512
DOWN_TN = 1024
RNN_SEGMENTS = 32
ROW_TM = 256
RNN_CHUNK = 512


def _params(sem):
    return pltpu.CompilerParams(dimension_semantics=sem, vmem_limit_bytes=VMEM_LIMIT)


def _ada_kernel(c_ref, *refs):
    w_refs, (b_ref, o_ref) = refs[:ADA_K_SPLIT], refs[ADA_K_SPLIT:]
    c = c_ref[...]
    c_act = (c * jax.nn.sigmoid(c)).astype(BF16)
    kc = w_refs[0].shape[0]
    acc = b_ref[...]
    for s, w_ref in enumerate(w_refs):
        acc = acc + jnp.dot(c_act[:, s * kc:(s + 1) * kc], w_ref[...].astype(BF16),
                            preferred_element_type=F32)
    o_ref[...] = acc


def _ada(c_pad, w_ada, b_ada):
    depth, d, n = w_ada.shape
    kc = d // ADA_K_SPLIT
    w_specs = [pl.BlockSpec((None, kc, ADA_TN), functools.partial(lambda l, j, s: (l, s, j), s=s))
               for s in range(ADA_K_SPLIT)]
    return pl.pallas_call(
        _ada_kernel,
        out_shape=jax.ShapeDtypeStruct((depth, SUBLANES, n), F32),
        grid=(depth, n // ADA_TN),
        in_specs=[pl.BlockSpec((SUBLANES, d), lambda l, j: (0, 0))] + w_specs
        + [pl.BlockSpec((None, 1, ADA_TN), lambda l, j: (l, 0, j))],
        out_specs=pl.BlockSpec((None, SUBLANES, ADA_TN), lambda l, j: (l, 0, j)),
        compiler_params=_params(("arbitrary", "arbitrary")),
        name="ada_mod",
    )(c_pad, *([w_ada] * ADA_K_SPLIT), b_ada.reshape(depth, 1, n))


def _permute_modulate_kernel(x_ref, sc_ref, sh_ref, x2_ref, u_ref):
    ns = x_ref.shape[0]
    d = x2_ref.shape[1]
    scale = 1.0 + sc_ref[...]
    shift = sh_ref[...]
    for jj in range(SUBLANES):
        xs = x_ref[:, jj * d:(jj + 1) * d]
        x2_ref[jj * ns:(jj + 1) * ns, :] = xs
        u_ref[jj * ns:(jj + 1) * ns, :] = (xs * scale + shift).astype(BF16)


def _permute_modulate(x, sc, sh):
    batch, seq, d = x.shape
    ns = RNN_SEGMENTS
    steps = seq // ns
    rows = SUBLANES * ns
    per_b = steps // SUBLANES
    vec = pl.BlockSpec((None, 1, d), lambda b, i: (b, 0, 0))
    out = pl.BlockSpec((rows, d), lambda b, i: (b * per_b + i, 0))
    return pl.pallas_call(
        _permute_modulate_kernel,
        out_shape=(jax.ShapeDtypeStruct((batch * seq, d), F32),
                   jax.ShapeDtypeStruct((batch * seq, d), BF16)),
        grid=(batch, per_b),
        in_specs=[pl.BlockSpec((None, ns, SUBLANES * d), lambda b, i: (b, 0, i)), vec, vec],
        out_specs=(out, out),
        compiler_params=_params(("arbitrary", "arbitrary")),
        name="permute_modulate",
    )(x.reshape(batch, ns, steps * d), sc, sh)


def _rms_rope(xh, gain, cos, sin_a, sin_b):
    r = lax.rsqrt(jnp.mean(xh * xh, axis=-1, keepdims=True) + RMS_EPS)
    xn = xh * r * gain
    return xn * cos + pltpu.roll(xn, 96, 1) * sin_a + pltpu.roll(xn, 32, 1) * sin_b


_Q_TILES = D_ATTN // INPROJ_TN
_K_TILE = _Q_TILES
_V_TILE = _K_TILE + 1
_R_TILE0 = _V_TILE + 1
_R_TILES = D_RNN // INPROJ_TN
_G_TILE0 = _R_TILE0 + _R_TILES
assert D_KV == INPROJ_TN and D_IN == (_G_TILE0 + _R_TILES) * INPROJ_TN


def _inproj_kernel(u_ref, w_ref, g_ref, cos_ref, sa_ref, sb_ref,
                   q_ref, k_ref, vt_ref, xr_ref, xg_ref):
    j = pl.program_id(1)
    z = jnp.dot(u_ref[...], w_ref[...], preferred_element_type=F32)
    heads = INPROJ_TN // HEAD_DIM

    def head(h):
        return z[:, h * HEAD_DIM:(h + 1) * HEAD_DIM]

    @pl.when(j < _Q_TILES)
    def _():
        cos, sa, sb = cos_ref[...], sa_ref[...], sb_ref[...]
        for h in range(heads):
            q_ref[h] = (_rms_rope(head(h), g_ref[0:1, :], cos, sa, sb) * Q_SCALE).astype(BF16)

    @pl.when(j == _K_TILE)
    def _():
        cos, sa, sb = cos_ref[...], sa_ref[...], sb_ref[...]
        for h in range(heads):
            k_ref[h] = _rms_rope(head(h), g_ref[1:2, :], cos, sa, sb).astype(BF16)

    @pl.when(j == _V_TILE)
    def _():
        for h in range(heads):
            vt_ref[h, 0:HEAD_DIM, :] = head(h).T.astype(BF16)
            vt_ref[h, HEAD_DIM:, :] = jnp.ones((VT_ROWS - HEAD_DIM, z.shape[0]), BF16)

    @pl.when((j >= _R_TILE0) & (j < _G_TILE0))
    def _():
        xr_ref[...] = z.astype(BF16)

    @pl.when(j >= _G_TILE0)
    def _():
        xg_ref[...] = z.astype(BF16)


def _inproj(u, w_in, qk_g, cos, sin_a, sin_b, seq):
    m, d = u.shape
    tm, tn = INPROJ_TM, INPROJ_TN
    per_b = seq // tm
    heads = tn // HEAD_DIM
    tab = pl.BlockSpec((tm, HEAD_DIM), lambda i, j: (i % per_b, 0))
    return pl.pallas_call(
        _inproj_kernel,
        out_shape=(
            jax.ShapeDtypeStruct((N_Q_HEADS, m, HEAD_DIM), BF16),
            jax.ShapeDtypeStruct((N_KV_HEADS, m, HEAD_DIM), BF16),
            jax.ShapeDtypeStruct((N_KV_HEADS, VT_ROWS, m), BF16),
            jax.ShapeDtypeStruct((m, D_RNN), BF16),
            jax.ShapeDtypeStruct((m, D_RNN), BF16),
        ),
        grid=(m // tm, D_IN // tn),
        in_specs=[
            pl.BlockSpec((tm, d), lambda i, j: (i, 0)),
            pl.BlockSpec((d, tn), lambda i, j: (0, j)),
            pl.BlockSpec((2, HEAD_DIM), lambda i, j: (0, 0)),
            tab, tab, tab,
        ],
        out_specs=(
            pl.BlockSpec((heads, tm, HEAD_DIM), lambda i, j: (jnp.minimum(j, _Q_TILES - 1), i, 0)),
            pl.BlockSpec((N_KV_HEADS, tm, HEAD_DIM), lambda i, j: (0, i, 0)),
            pl.BlockSpec((N_KV_HEADS, VT_ROWS, tm), lambda i, j: (0, 0, i)),
            pl.BlockSpec((tm, tn), lambda i, j: (i, jnp.clip(j - _R_TILE0, 0, _R_TILES - 1))),
            pl.BlockSpec((tm, tn), lambda i, j: (i, jnp.clip(j - _G_TILE0, 0, _R_TILES - 1))),
        ),
        compiler_params=_params(("arbitrary", "arbitrary")),
        name="in_proj",
    )(u, w_in, qk_g, cos, sin_a, sin_b)


def _attn_kernel(q_ref, k_ref, vt_ref, g_ref, o_ref, m_s, acc_s):
    j = pl.program_id(2)
    tq = q_ref.shape[1]
    tk = k_ref.shape[1]
    rows = tk // ATTN_SCORE_SPLIT

    @pl.when(j == 0)
    def _():
        m_s[...] = jnp.full(m_s.shape, -jnp.inf, F32)
        acc_s[...] = jnp.zeros(acc_s.shape, F32)

    def scores(kh, g):
        q = q_ref[Q_PER_KV * kh + g]
        return [lax.dot_general(k_ref[kh, r * rows:(r + 1) * rows, :], q, (((1,), (1,)), ((), ())),
                                preferred_element_type=F32) for r in range(ATTN_SCORE_SPLIT)]

    def update(kh, g, s_parts):
        cols = slice(g * tq, (g + 1) * tq)
        m_old = m_s[kh:kh + 1, cols]
        m_new = m_old
        for s_t in s_parts:
            m_new = jnp.maximum(m_new, jnp.max(s_t, axis=0, keepdims=True))
        alpha = jnp.exp2(m_old - m_new)
        p = jnp.concatenate([jnp.exp2(s_t - m_new).astype(BF16) for s_t in s_parts], axis=0)
        pv = jnp.dot(vt_ref[kh], p, preferred_element_type=F32)
        acc_s[kh, :, cols] = alpha * acc_s[kh, :, cols] + pv
        m_s[kh:kh + 1, cols] = m_new

    blocks = [(kh, g) for kh in range(N_KV_HEADS) for g in range(Q_PER_KV)]
    pending = [scores(*blk) for blk in blocks[:ATTN_LOOKAHEAD]]
    for c, (kh, g) in enumerate(blocks):
        if c + ATTN_LOOKAHEAD < len(blocks):
            pending.append(scores(*blocks[c + ATTN_LOOKAHEAD]))
        update(kh, g, pending.pop(0))

    @pl.when(j == pl.num_programs(2) - 1)
    def _():
        outs = []
        ss = jnp.zeros((1, tq), F32)
        for kh in range(N_KV_HEADS):
            o = acc_s[kh, 0:HEAD_DIM, :] / acc_s[kh, HEAD_DIM:HEAD_DIM + 1, :]
            sq = jnp.sum(o * o, axis=0, keepdims=True)
            for g in range(Q_PER_KV):
                ss = ss + sq[:, g * tq:(g + 1) * tq]
            outs.append(o)
        r = lax.rsqrt(ss * (1.0 / D_ATTN) + RMS_EPS)
        for kh in range(N_KV_HEADS):
            for g in range(Q_PER_KV):
                h = kh * Q_PER_KV + g
                oh = (outs[kh][:, g * tq:(g + 1) * tq] * r).T
                o_ref[:, h * HEAD_DIM:(h + 1) * HEAD_DIM] = (
                    oh * g_ref[:, h * HEAD_DIM:(h + 1) * HEAD_DIM]).astype(BF16)


def _attention(q, k, vt, out_g_attn, batch, seq):
    tq, tk = ATTN_TQ, ATTN_TK
    nq, nk = seq // tq, seq // tk
    m = batch * seq
    return pl.pallas_call(
        _attn_kernel,
        out_shape=jax.ShapeDtypeStruct((m, D_ATTN), BF16),
        grid=(batch, nq, nk),
        in_specs=[
            pl.BlockSpec((N_Q_HEADS, tq, HEAD_DIM), lambda b, i, j: (0, b * nq + i, 0)),
            pl.BlockSpec((N_KV_HEADS, tk, HEAD_DIM), lambda b, i, j: (0, b * nk + j, 0)),
            pl.BlockSpec((N_KV_HEADS, VT_ROWS, tk), lambda b, i, j: (0, 0, b * nk + j)),
            pl.BlockSpec((1, D_ATTN), lambda b, i, j: (0, 0)),
        ],
        out_specs=pl.BlockSpec((tq, D_ATTN), lambda b, i, j: (b * nq + i, 0)),
        scratch_shapes=[
            pltpu.VMEM((SUBLANES, Q_PER_KV * tq), F32),
            pltpu.VMEM((N_KV_HEADS, VT_ROWS, Q_PER_KV * tq), F32),
        ],
        compiler_params=_params(("arbitrary", "arbitrary", "arbitrary")),
        name="attention",
    )(q, k, vt, out_g_attn)


def _gelu_tanh(x):
    return 0.5 * x * (1.0 + jnp.tanh(math.sqrt(2.0 / math.pi) * (x + 0.044715 * (x * x * x))))


def _softplus(x):
    return jnp.maximum(x, 0.0) + jnp.log1p(jnp.exp(-jnp.abs(x)))


def _rglru_kernel(xr_ref, xg_ref, cw_ref, cb_ref, wa_ref, ba_ref, wi_ref, bi_ref, lam_ref,
                  y_ref, xe_s, a_s, u_s):
    seq = xr_ref.shape[0]
    ns = RNN_SEGMENTS
    steps = seq // ns
    t = RNN_CHUNK
    n_chunks = seq // t
    chains = ns // SUBLANES

    cw = cw_ref[...]
    cb = cb_ref[...]
    coef = -RG_C * _softplus(-lam_ref[...])
    w_a = [wa_ref[d].astype(BF16) for d in range(2)]
    w_i = [wi_ref[d].astype(BF16) for d in range(2)]

    seg_id = lax.broadcasted_iota(jnp.int32, (ns, LANES), 0)
    xe_s[pl.ds(0, ns), :] = jnp.where(
        seg_id == 0, 0.0, pltpu.roll(xr_ref[pl.ds(seq - ns, ns), :].astype(F32), 1, 0))
    for k in range(2):
        xe_s[pl.ds(ns + seq + k * ns, ns), :] = jnp.where(
            seg_id == ns - 1, 0.0, pltpu.roll(xr_ref[pl.ds(k * ns, ns), :].astype(F32), ns - 1, 0))

    def stage(c, carry):
        t0 = pl.multiple_of(c * t, t)
        xe_s[pl.ds(ns + t0, t), :] = xr_ref[pl.ds(t0, t), :].astype(F32)
        return carry

    lax.fori_loop(0, n_chunks, stage, 0)

    def gates(c, carry):
        t0 = pl.multiple_of(c * t, t)
        xc = cb
        for tap in range(CONV_W):
            xc = xc + cw[tap:tap + 1, :] * xe_s[pl.ds(t0 + tap * ns, t), :]
        xcb = xc.astype(BF16)
        for d in range(2):
            r = jax.nn.sigmoid(jnp.dot(xcb, w_a[d], preferred_element_type=F32) + ba_ref[d:d + 1, :])
            i = jax.nn.sigmoid(jnp.dot(xcb, w_i[d], preferred_element_type=F32) + bi_ref[d:d + 1, :])
            log_a = coef[d:d + 1, :] * r
            a = jnp.exp(log_a)
            a_s[d, pl.ds(t0, t), :] = a
            u_s[d, pl.ds(t0, t), :] = jnp.sqrt(-jnp.tanh(log_a) * (a * a + 1.0)) * i * xc
        return carry

    lax.fori_loop(0, n_chunks, gates, 0)

    def seg_rows(g, j):
        return pl.ds(pl.multiple_of(j * ns + g * SUBLANES, SUBLANES), SUBLANES)

    def local(j, carry):
        out = []
        for d in range(2):
            jj = j if d == 0 else steps - 1 - j
            for g in range(chains):
                h, p = carry[d * chains + g]
                a = a_s[d, seg_rows(g, jj), :]
                out.append((a * h + u_s[d, seg_rows(g, jj), :], a * p))
        return tuple(out)

    zero = jnp.zeros((SUBLANES, LANES), F32)
    one = jnp.ones((SUBLANES, LANES), F32)
    ends = lax.fori_loop(0, steps, local, tuple((zero, one) for _ in range(2 * chains)), unroll=8)

    starts = []
    for d in range(2):
        h_end = jnp.concatenate([ends[d * chains + g][0] for g in range(chains)], axis=0)
        p_end = jnp.concatenate([ends[d * chains + g][1] for g in range(chains)], axis=0)
        order = range(ns) if d == 0 else range(ns - 1, -1, -1)
        carry = jnp.zeros((1, LANES), F32)
        rows = [None] * ns
        for s in order:
            rows[s] = carry
            carry = p_end[s:s + 1, :] * carry + h_end[s:s + 1, :]
        init = jnp.concatenate(rows, axis=0)
        starts.extend(init[g * SUBLANES:(g + 1) * SUBLANES] for g in range(chains))

    def final(j, carry):
        out = []
        for d in range(2):
            jj = j if d == 0 else steps - 1 - j
            for g in range(chains):
                h = a_s[d, seg_rows(g, jj), :] * carry[d * chains + g] + u_s[d, seg_rows(g, jj), :]
                u_s[d, seg_rows(g, jj), :] = h
                out.append(h)
        return tuple(out)

    lax.fori_loop(0, steps, final, tuple(starts), unroll=8)

    def combine(c, carry):
        t0 = pl.multiple_of(c * t, t)
        h = u_s[0, pl.ds(t0, t), :] + u_s[1, pl.ds(t0, t), :]
        y_ref[pl.ds(t0, t), :] = _gelu_tanh(xg_ref[pl.ds(t0, t), :].astype(F32)) * h
        return carry

    lax.fori_loop(0, n_chunks, combine, 0)


def _rglru(xr, xg, conv_w, conv_b, w_a, b_a, w_i, b_i, lam, batch, seq):
    m, d_rnn = xr.shape
    blk = RNN_BLOCK
    assert seq % RNN_CHUNK == 0 and RNN_CHUNK % RNN_SEGMENTS == 0 and RNN_SEGMENTS % SUBLANES == 0
    slab = pl.BlockSpec((seq, blk), lambda b, n: (b, n))
    vec2 = pl.BlockSpec((2, blk), lambda b, n: (0, n))
    wspec = pl.BlockSpec((2, None, blk, blk), lambda b, n: (0, n, 0, 0))
    return pl.pallas_call(
        _rglru_kernel,
        out_shape=jax.ShapeDtypeStruct((m, d_rnn), F32),
        grid=(batch, d_rnn // blk),
        in_specs=[
            slab, slab,
            pl.BlockSpec((CONV_W, blk), lambda b, n: (0, n)),
            pl.BlockSpec((1, blk), lambda b, n: (0, n)),
            wspec, vec2, wspec, vec2, vec2,
        ],
        out_specs=slab,
        scratch_shapes=[pltpu.VMEM((seq + (CONV_W - 1) * RNN_SEGMENTS, blk), F32),
                        pltpu.VMEM((2, seq, blk), F32), pltpu.VMEM((2, seq, blk), F32)],
        compiler_params=_params(("arbitrary", "arbitrary")),
        name="rg_lru",
    )(xr, xg, conv_w, conv_b.reshape(1, d_rnn), w_a, b_a, w_i, b_i, lam)


def _rms_rows_kernel(y_ref, g_ref, o_ref):
    y = y_ref[...]
    r = lax.rsqrt(jnp.mean(y * y, axis=-1, keepdims=True) + RMS_EPS)
    o_ref[...] = (y * r * g_ref[...]).astype(BF16)


def _rms_rows(y, gain):
    m, d = y.shape
    return pl.pallas_call(
        _rms_rows_kernel,
        out_shape=jax.ShapeDtypeStruct((m, d), BF16),
        grid=(m // ROW_TM,),
        in_specs=[pl.BlockSpec((ROW_TM, d), lambda i: (i, 0)), pl.BlockSpec((1, d), lambda i: (0, 0))],
        out_specs=pl.BlockSpec((ROW_TM, d), lambda i: (i, 0)),
        compiler_params=_params(("arbitrary",)),
        name="rnn_rms",
    )(y, gain)


def _outproj_kernel(a1_ref, a2_ref, w_ref, o_ref):
    k1 = a1_ref.shape[1]
    o_ref[...] = (jnp.dot(a1_ref[...], w_ref[0:k1, :], preferred_element_type=F32)
                  + jnp.dot(a2_ref[...], w_ref[k1:, :], preferred_element_type=F32)).astype(o_ref.dtype)


def _outproj(a1, a2, w_o):
    m, k1 = a1.shape
    k2 = a2.shape[1]
    n = w_o.shape[1]
    return pl.pallas_call(
        _outproj_kernel,
        out_shape=jax.ShapeDtypeStruct((m, n), BF16),
        grid=(m // MM_TM, n // MM_TN),
        in_specs=[
            pl.BlockSpec((MM_TM, k1), lambda i, j: (i, 0)),
            pl.BlockSpec((MM_TM, k2), lambda i, j: (i, 0)),
            pl.BlockSpec((k1 + k2, MM_TN), lambda i, j: (0, j)),
        ],
        out_specs=pl.BlockSpec((MM_TM, MM_TN), lambda i, j: (i, j)),
        compiler_params=_params(("arbitrary", "arbitrary")),
        name="out_proj",
    )(a1, a2, w_o)


def _deepnorm_kernel(x_ref, y_ref, gate_ref, g_ref, b_ref, *rest, alpha, modulate):
    s = alpha * x_ref[...] + gate_ref[...] * y_ref[...].astype(F32)
    mu = jnp.mean(s, axis=-1, keepdims=True)
    sc = s - mu
    var = jnp.mean(sc * sc, axis=-1, keepdims=True)
    out = sc * lax.rsqrt(var + LN_EPS) * g_ref[...] + b_ref[...]
    if modulate:
        sc_ref, sh_ref, o_ref, u_ref = rest
        u_ref[...] = (out * (1.0 + sc_ref[...]) + sh_ref[...]).astype(BF16)
        o_ref[...] = out
    else:
        (o_ref,) = rest
        ns, d = o_ref.shape[0], out.shape[1]
        for jj in range(SUBLANES):
            o_ref[:, jj * d:(jj + 1) * d] = out[jj * ns:(jj + 1) * ns, :]


def _deepnorm(x2, y, gate, ln_g, ln_b, nxt, alpha, seq):
    m, d = x2.shape
    per_b = seq // ROW_TM
    rows = pl.BlockSpec((ROW_TM, d), lambda i: (i, 0))
    vec_b = pl.BlockSpec((None, 1, d), lambda i: (i // per_b, 0, 0))
    vec = pl.BlockSpec((1, d), lambda i: (0, 0))
    modulate = nxt is not None
    in_specs = [rows, rows, vec_b, vec, vec]
    args = [x2, y, gate, ln_g.reshape(1, d), ln_b.reshape(1, d)]
    if modulate:
        in_specs += [vec_b, vec_b]
        args += list(nxt)
        out_shape = [jax.ShapeDtypeStruct((m, d), F32), jax.ShapeDtypeStruct((m, d), BF16)]
        out_specs = [rows, rows]
    else:
        ns = RNN_SEGMENTS
        assert ROW_TM == SUBLANES * ns
        out_shape = [jax.ShapeDtypeStruct((m // seq, ns, (seq // ns) * d), F32)]
        out_specs = [pl.BlockSpec((None, ns, SUBLANES * d), lambda i: (i // per_b, 0, i % per_b))]
    res = pl.pallas_call(
        functools.partial(_deepnorm_kernel, alpha=alpha, modulate=modulate),
        out_shape=tuple(out_shape),
        grid=(m // ROW_TM,),
        in_specs=in_specs,
        out_specs=tuple(out_specs),
        compiler_params=_params(("arbitrary",)),
        name="deepnorm",
    )(*args)
    return res if modulate else (res[0], None)


def _ffn_in_kernel(u_ref, wg_ref, wu_ref, h_ref):
    u = u_ref[...]
    gate = jnp.dot(u, wg_ref[...], preferred_element_type=F32)
    up = jnp.dot(u, wu_ref[...], preferred_element_type=F32)
    h_ref[...] = (gate * jax.nn.sigmoid(gate) * up).astype(BF16)


def _ffn_in(u, w_gate, w_up):
    m, d = u.shape
    d_ff = w_gate.shape[1]
    return pl.pallas_call(
        _ffn_in_kernel,
        out_shape=jax.ShapeDtypeStruct((m, d_ff), BF16),
        grid=(m // MM_TM, pl.cdiv(d_ff, FFN_TN)),
        in_specs=[
            pl.BlockSpec((MM_TM, d), lambda i, j: (i, 0)),
            pl.BlockSpec((d, FFN_TN), lambda i, j: (0, j)),
            pl.BlockSpec((d, FFN_TN), lambda i, j: (0, j)),
        ],
        out_specs=pl.BlockSpec((MM_TM, FFN_TN), lambda i, j: (i, j)),
        compiler_params=_params(("arbitrary", "arbitrary")),
        name="ffn_in",
    )(u, w_gate, w_up)


def _ffn_down_kernel(h_ref, w_ref, o_ref):
    o_ref[...] = jnp.dot(h_ref[...], w_ref[...], preferred_element_type=F32).astype(o_ref.dtype)


def _ffn_down(h, w_down):
    m, kdim = h.shape
    n = w_down.shape[1]
    return pl.pallas_call(
        _ffn_down_kernel,
        out_shape=jax.ShapeDtypeStruct((m, n), BF16),
        grid=(n // DOWN_TN, m // DOWN_TM),
        in_specs=[
            pl.BlockSpec((DOWN_TM, kdim), lambda j, i: (i, 0)),
            pl.BlockSpec((kdim, DOWN_TN), lambda j, i: (0, j), pipeline_mode=pl.Buffered(1)),
        ],
        out_specs=pl.BlockSpec((DOWN_TM, DOWN_TN), lambda j, i: (i, j)),
        compiler_params=_params(("arbitrary", "arbitrary")),
        name="ffn_down",
    )(h, w_down)


def _rope_tables(seq):
    pos = jnp.arange(seq, dtype=jnp.int32)
    row = (pos // GRID_W).astype(F32)
    col = (pos % GRID_W).astype(F32)
    inv_freq = jnp.power(ROPE_THETA, -jnp.arange(0, ROPE_AXIS_DIM, 2, dtype=F32) / ROPE_AXIS_DIM)
    ang_r = row[:, None] * inv_freq[None, :]
    ang_c = col[:, None] * inv_freq[None, :]
    cos_r, sin_r, cos_c, sin_c = jnp.cos(ang_r), jnp.sin(ang_r), jnp.cos(ang_c), jnp.sin(ang_c)
    zero = jnp.zeros_like(sin_r)
    cos = jnp.concatenate([cos_r, cos_r, cos_c, cos_c], axis=-1)
    sin_a = jnp.concatenate([-sin_r, zero, -sin_c, zero], axis=-1)
    sin_b = jnp.concatenate([zero, sin_r, zero, sin_c], axis=-1)
    return cos, sin_a, sin_b


def kernel(x, c, w_ada, b_ada, w_in, qk_norm_g, conv_w, conv_b, rg_w_a, rg_b_a, rg_w_i, rg_b_i,
           rg_lam, mix_out_g, w_o, ln_g, ln_b, w_ffn_in, w_down):
    batch, seq, d = x.shape
    depth = w_ada.shape[0]
    m = batch * seq
    alpha = (2 * depth) ** 0.25
    assert d == D_MODEL and seq % ATTN_TK == 0 and seq % INPROJ_TM == 0 and m % MM_TM == 0
    assert m % DOWN_TM == 0 and d % DOWN_TN == 0

    ns = RNN_SEGMENTS

    def interleave(a):
        lead = a.shape[:-2]
        return a.reshape(*lead, ns, seq // ns, a.shape[-1]).swapaxes(-3, -2).reshape(a.shape)

    cos, sin_a, sin_b = (interleave(tab) for tab in _rope_tables(seq))
    c_pad = jnp.pad(c, ((0, SUBLANES - batch), (0, 0)))
    mod = _ada(c_pad, w_ada, b_ada)[:, :batch].reshape(depth, batch, 6, 1, d)

    w_in_b = w_in.astype(BF16)
    w_o_b = w_o.astype(BF16)
    w_gate_b = w_ffn_in[:, :, :D_FF].astype(BF16)
    w_up_b = w_ffn_in[:, :, D_FF:].astype(BF16)
    w_down_b = w_down.astype(BF16)

    x2, u = _permute_modulate(x, mod[0, :, 1], mod[0, :, 0])
    for l in range(depth):
        sh2, sc2, g1, g2 = mod[l, :, 3], mod[l, :, 4], mod[l, :, 2], mod[l, :, 5]
        q, k, vt, xr, xg = _inproj(u, w_in_b[l], qk_norm_g[l], cos, sin_a, sin_b, seq)
        y_attn = _attention(q, k, vt, mix_out_g[l, :D_ATTN].reshape(1, D_ATTN), batch, seq)
        y_rnn = _rglru(xr, xg, conv_w[l], conv_b[l], rg_w_a[l], rg_b_a[l], rg_w_i[l], rg_b_i[l],
                       rg_lam[l], batch, seq)
        y_rnn = _rms_rows(y_rnn, mix_out_g[l, D_ATTN:].reshape(1, D_RNN))
        y = _outproj(y_attn, y_rnn, w_o_b[l])
        x2, u = _deepnorm(x2, y, g1, ln_g[l, 0], ln_b[l, 0], (sc2, sh2), alpha, seq)
        h = _ffn_in(u, w_gate_b[l], w_up_b[l])
        y = _ffn_down(h, w_down_b[l])
        nxt = (mod[l + 1, :, 1], mod[l + 1, :, 0]) if l + 1 < depth else None
        x2, u = _deepnorm(x2, y, g2, ln_g[l, 1], ln_b[l, 1], nxt, alpha, seq)
    return x2.reshape(batch, seq, d)
```

```python
import functools
import math

import jax
import jax.numpy as jnp
from jax import lax
from jax.experimental import pallas as pl
from jax.experimental.pallas import tpu as pltpu

D_MODEL = 4096
GRID_W = 64
HEAD_DIM = 128
D_ATTN = D_MODEL // 2
N_Q_HEADS = D_ATTN // HEAD_DIM
N_KV_HEADS = 4
Q_PER_KV = N_Q_HEADS // N_KV_HEADS
D_KV = N_KV_HEADS * HEAD_DIM
D_RNN = D_MODEL - D_ATTN
RNN_BLOCK = 128
N_RNN_BLOCKS = D_RNN // RNN_BLOCK
D_IN = D_ATTN + 2 * D_KV + 2 * D_RNN
CONV_W = 4
RG_C = 8.0
ROPE_THETA = 10000.0
ROPE_AXIS_DIM = HEAD_DIM // 2
D_FF = (-(-8 * D_MODEL // (3 * 256))) * 256
LN_EPS = 1e-5
RMS_EPS = 1e-6
Q_SCALE = HEAD_DIM ** -0.5 * math.log2(math.e)

V7X_VMEM_BYTES = 64 * 1024 * 1024
VMEM_LIMIT = V7X_VMEM_BYTES - 8 * 1024 * 1024
LANES = 128
SUBLANES = 8

BF16 = jnp.bfloat16
F32 = jnp.float32

ADA_TN = 1024
ADA_K_SPLIT = 4
INPROJ_TM = 1024
INPROJ_TN = 512
ATTN_TQ = 512
ATTN_TK = 1024
ATTN_SCORE_SPLIT = 4
ATTN_LOOKAHEAD = 2
VT_ROWS = HEAD_DIM + 16
MM_TM = 1024
MM_TN = 1024
FFN_TN = 512
DOWN_TM = 512
DOWN_TN = 1024
RNN_SEGMENTS = 32
ROW_TM = 256
RNN_CHUNK = 512


def _params(sem):
    return pltpu.CompilerParams(dimension_semantics=sem, vmem_limit_bytes=VMEM_LIMIT)


def _ada_kernel(c_ref, *refs):
    w_refs, (b_ref, o_ref) = refs[:ADA_K_SPLIT], refs[ADA_K_SPLIT:]
    c = c_ref[...]
    c_act = (c * jax.nn.sigmoid(c)).astype(BF16)
    kc = w_refs[0].shape[0]
    acc = b_ref[...]
    for s, w_ref in enumerate(w_refs):
        acc = acc + jnp.dot(c_act[:, s * kc:(s + 1) * kc], w_ref[...].astype(BF16),
                            preferred_element_type=F32)
    o_ref[...] = acc


def _ada(c_pad, w_ada, b_ada):
    depth, d, n = w_ada.shape
    kc = d // ADA_K_SPLIT
    w_specs = [pl.BlockSpec((None, kc, ADA_TN), functools.partial(lambda l, j, s: (l, s, j), s=s))
               for s in range(ADA_K_SPLIT)]
    return pl.pallas_call(
        _ada_kernel,
        out_shape=jax.ShapeDtypeStruct((depth, SUBLANES, n), F32),
        grid=(depth, n // ADA_TN),
        in_specs=[pl.BlockSpec((SUBLANES, d), lambda l, j: (0, 0))] + w_specs
        + [pl.BlockSpec((None, 1, ADA_TN), lambda l, j: (l, 0, j))],
        out_specs=pl.BlockSpec((None, SUBLANES, ADA_TN), lambda l, j: (l, 0, j)),
        compiler_params=_params(("arbitrary", "arbitrary")),
        name="ada_mod",
    )(c_pad, *([w_ada] * ADA_K_SPLIT), b_ada.reshape(depth, 1, n))


def _modulate_kernel(x_ref, sc_ref, sh_ref, u_ref):
    u_ref[...] = (x_ref[...] * (1.0 + sc_ref[...]) + sh_ref[...]).astype(BF16)


def _modulate(x2, sc, sh, seq):
    m, d = x2.shape
    per_b = seq // ROW_TM
    vec = pl.BlockSpec((None, 1, d), lambda i: (i // per_b, 0, 0))
    return pl.pallas_call(
        _modulate_kernel,
        out_shape=jax.ShapeDtypeStruct((m, d), BF16),
        grid=(m // ROW_TM,),
        in_specs=[pl.BlockSpec((ROW_TM, d), lambda i: (i, 0)), vec, vec],
        out_specs=pl.BlockSpec((ROW_TM, d), lambda i: (i, 0)),
        compiler_params=_params(("arbitrary",)),
        name="modulate",
    )(x2, sc, sh)


def _rms_rope(xh, gain, cos, sin_a, sin_b):
    r = lax.rsqrt(jnp.mean(xh * xh, axis=-1, keepdims=True) + RMS_EPS)
    xn = xh * r * gain
    return xn * cos + pltpu.roll(xn, 96, 1) * sin_a + pltpu.roll(xn, 32, 1) * sin_b


_Q_TILES = D_ATTN // INPROJ_TN
_K_TILE = _Q_TILES
_V_TILE = _K_TILE + 1
_R_TILE0 = _V_TILE + 1
_R_TILES = D_RNN // INPROJ_TN
_G_TILE0 = _R_TILE0 + _R_TILES
assert D_KV == INPROJ_TN and D_IN == (_G_TILE0 + _R_TILES) * INPROJ_TN


def _inproj_kernel(u_ref, w_ref, g_ref, cos_ref, sa_ref, sb_ref,
                   q_ref, k_ref, vt_ref, xr_ref, xg_ref):
    j = pl.program_id(1)
    z = jnp.dot(u_ref[...], w_ref[...], preferred_element_type=F32)
    heads = INPROJ_TN // HEAD_DIM

    def head(h):
        return z[:, h * HEAD_DIM:(h + 1) * HEAD_DIM]

    @pl.when(j < _Q_TILES)
    def _():
        cos, sa, sb = cos_ref[...], sa_ref[...], sb_ref[...]
        for h in range(heads):
            q_ref[h] = (_rms_rope(head(h), g_ref[0:1, :], cos, sa, sb) * Q_SCALE).astype(BF16)

    @pl.when(j == _K_TILE)
    def _():
        cos, sa, sb = cos_ref[...], sa_ref[...], sb_ref[...]
        for h in range(heads):
            k_ref[h] = _rms_rope(head(h), g_ref[1:2, :], cos, sa, sb).astype(BF16)

    @pl.when(j == _V_TILE)
    def _():
        for h in range(heads):
            vt_ref[h, 0:HEAD_DIM, :] = head(h).T.astype(BF16)
            vt_ref[h, HEAD_DIM:, :] = jnp.ones((VT_ROWS - HEAD_DIM, z.shape[0]), BF16)

    @pl.when((j >= _R_TILE0) & (j < _G_TILE0))
    def _():
        xr_ref[...] = z.astype(BF16)

    @pl.when(j >= _G_TILE0)
    def _():
        xg_ref[...] = z.astype(BF16)


def _inproj(u, w_in, qk_g, cos, sin_a, sin_b, seq):
    m, d = u.shape
    tm, tn = INPROJ_TM, INPROJ_TN
    per_b = seq // tm
    heads = tn // HEAD_DIM
    tab = pl.BlockSpec((tm, HEAD_DIM), lambda i, j: (i % per_b, 0))
    return pl.pallas_call(
        _inproj_kernel,
        out_shape=(
            jax.ShapeDtypeStruct((N_Q_HEADS, m, HEAD_DIM), BF16),
            jax.ShapeDtypeStruct((N_KV_HEADS, m, HEAD_DIM), BF16),
            jax.ShapeDtypeStruct((N_KV_HEADS, VT_ROWS, m), BF16),
            jax.ShapeDtypeStruct((m, D_RNN), BF16),
            jax.ShapeDtypeStruct((m, D_RNN), BF16),
        ),
        grid=(m // tm, D_IN // tn),
        in_specs=[
            pl.BlockSpec((tm, d), lambda i, j: (i, 0)),
            pl.BlockSpec((d, tn), lambda i, j: (0, j)),
            pl.BlockSpec((2, HEAD_DIM), lambda i, j: (0, 0)),
            tab, tab, tab,
        ],
        out_specs=(
            pl.BlockSpec((heads, tm, HEAD_DIM), lambda i, j: (jnp.minimum(j, _Q_TILES - 1), i, 0)),
            pl.BlockSpec((N_KV_HEADS, tm, HEAD_DIM), lambda i, j: (0, i, 0)),
            pl.BlockSpec((N_KV_HEADS, VT_ROWS, tm), lambda i, j: (0, 0, i)),
            pl.BlockSpec((tm, tn), lambda i, j: (i, jnp.clip(j - _R_TILE0, 0, _R_TILES - 1))),
            pl.BlockSpec((tm, tn), lambda i, j: (i, jnp.clip(j - _G_TILE0, 0, _R_TILES - 1))),
        ),
        compiler_params=_params(("arbitrary", "arbitrary")),
        name="in_proj",
    )(u, w_in, qk_g, cos, sin_a, sin_b)


def _attn_kernel(q_ref, k_ref, vt_ref, g_ref, o_ref, m_s, acc_s):
    j = pl.program_id(2)
    tq = q_ref.shape[1]
    tk = k_ref.shape[1]
    rows = tk // ATTN_SCORE_SPLIT

    @pl.when(j == 0)
    def _():
        m_s[...] = jnp.full(m_s.shape, -jnp.inf, F32)
        acc_s[...] = jnp.zeros(acc_s.shape, F32)

    def scores(kh, g):
        q = q_ref[Q_PER_KV * kh + g]
        return [lax.dot_general(k_ref[kh, r * rows:(r + 1) * rows, :], q, (((1,), (1,)), ((), ())),
                                preferred_element_type=F32) for r in range(ATTN_SCORE_SPLIT)]

    def update(kh, g, s_parts):
        cols = slice(g * tq, (g + 1) * tq)
        m_old = m_s[kh:kh + 1, cols]
        m_new = m_old
        for s_t in s_parts:
            m_new = jnp.maximum(m_new, jnp.max(s_t, axis=0, keepdims=True))
        alpha = jnp.exp2(m_old - m_new)
        p = jnp.concatenate([jnp.exp2(s_t - m_new).astype(BF16) for s_t in s_parts], axis=0)
        pv = jnp.dot(vt_ref[kh], p, preferred_element_type=F32)
        acc_s[kh, :, cols] = alpha * acc_s[kh, :, cols] + pv
        m_s[kh:kh + 1, cols] = m_new

    blocks = [(kh, g) for kh in range(N_KV_HEADS) for g in range(Q_PER_KV)]
    pending = [scores(*blk) for blk in blocks[:ATTN_LOOKAHEAD]]
    for c, (kh, g) in enumerate(blocks):
        if c + ATTN_LOOKAHEAD < len(blocks):
            pending.append(scores(*blocks[c + ATTN_LOOKAHEAD]))
        update(kh, g, pending.pop(0))

    @pl.when(j == pl.num_programs(2) - 1)
    def _():
        outs = []
        ss = jnp.zeros((1, tq), F32)
        for kh in range(N_KV_HEADS):
            o = acc_s[kh, 0:HEAD_DIM, :] / acc_s[kh, HEAD_DIM:HEAD_DIM + 1, :]
            sq = jnp.sum(o * o, axis=0, keepdims=True)
            for g in range(Q_PER_KV):
                ss = ss + sq[:, g * tq:(g + 1) * tq]
            outs.append(o)
        r = lax.rsqrt(ss * (1.0 / D_ATTN) + RMS_EPS)
        for kh in range(N_KV_HEADS):
            for g in range(Q_PER_KV):
                h = kh * Q_PER_KV + g
                oh = (outs[kh][:, g * tq:(g + 1) * tq] * r).T
                o_ref[:, h * HEAD_DIM:(h + 1) * HEAD_DIM] = (
                    oh * g_ref[:, h * HEAD_DIM:(h + 1) * HEAD_DIM]).astype(BF16)


def _attention(q, k, vt, out_g_attn, batch, seq):
    tq, tk = ATTN_TQ, ATTN_TK
    nq, nk = seq // tq, seq // tk
    m = batch * seq
    return pl.pallas_call(
        _attn_kernel,
        out_shape=jax.ShapeDtypeStruct((m, D_ATTN), BF16),
        grid=(batch, nq, nk),
        in_specs=[
            pl.BlockSpec((N_Q_HEADS, tq, HEAD_DIM), lambda b, i, j: (0, b * nq + i, 0)),
            pl.BlockSpec((N_KV_HEADS, tk, HEAD_DIM), lambda b, i, j: (0, b * nk + j, 0)),
            pl.BlockSpec((N_KV_HEADS, VT_ROWS, tk), lambda b, i, j: (0, 0, b * nk + j)),
            pl.BlockSpec((1, D_ATTN), lambda b, i, j: (0, 0)),
        ],
        out_specs=pl.BlockSpec((tq, D_ATTN), lambda b, i, j: (b * nq + i, 0)),
        scratch_shapes=[
            pltpu.VMEM((SUBLANES, Q_PER_KV * tq), F32),
            pltpu.VMEM((N_KV_HEADS, VT_ROWS, Q_PER_KV * tq), F32),
        ],
        compiler_params=_params(("arbitrary", "arbitrary", "arbitrary")),
        name="attention",
    )(q, k, vt, out_g_attn)


def _gelu_tanh(x):
    return 0.5 * x * (1.0 + jnp.tanh(math.sqrt(2.0 / math.pi) * (x + 0.044715 * (x * x * x))))


def _softplus(x):
    return jnp.maximum(x, 0.0) + jnp.log1p(jnp.exp(-jnp.abs(x)))


def _rglru_kernel(xr_ref, xg_ref, cw_ref, cb_ref, wa_ref, ba_ref, wi_ref, bi_ref, lam_ref,
                  y_ref, xe_s, a_s, u_s):
    seq = xr_ref.shape[0]
    ns = RNN_SEGMENTS
    steps = seq // ns
    t = RNN_CHUNK
    n_chunks = seq // t
    chains = ns // SUBLANES

    cw = cw_ref[...]
    cb = cb_ref[...]
    coef = -RG_C * _softplus(-lam_ref[...])
    w_a = [wa_ref[d].astype(BF16) for d in range(2)]
    w_i = [wi_ref[d].astype(BF16) for d in range(2)]

    seg_id = lax.broadcasted_iota(jnp.int32, (ns, LANES), 0)
    xe_s[pl.ds(0, ns), :] = jnp.where(
        seg_id == 0, 0.0, pltpu.roll(xr_ref[pl.ds(seq - ns, ns), :].astype(F32), 1, 0))
    for k in range(2):
        xe_s[pl.ds(ns + seq + k * ns, ns), :] = jnp.where(
            seg_id == ns - 1, 0.0, pltpu.roll(xr_ref[pl.ds(k * ns, ns), :].astype(F32), ns - 1, 0))

    def stage(c, carry):
        t0 = pl.multiple_of(c * t, t)
        xe_s[pl.ds(ns + t0, t), :] = xr_ref[pl.ds(t0, t), :].astype(F32)
        return carry

    lax.fori_loop(0, n_chunks, stage, 0)

    def gates(c, carry):
        t0 = pl.multiple_of(c * t, t)
        xc = cb
        for tap in range(CONV_W):
            xc = xc + cw[tap:tap + 1, :] * xe_s[pl.ds(t0 + tap * ns, t), :]
        xcb = xc.astype(BF16)
        for d in range(2):
            r = jax.nn.sigmoid(jnp.dot(xcb, w_a[d], preferred_element_type=F32) + ba_ref[d:d + 1, :])
            i = jax.nn.sigmoid(jnp.dot(xcb, w_i[d], preferred_element_type=F32) + bi_ref[d:d + 1, :])
            log_a = coef[d:d + 1, :] * r
            a = jnp.exp(log_a)
            a_s[d, pl.ds(t0, t), :] = a
            u_s[d, pl.ds(t0, t), :] = jnp.sqrt(-jnp.tanh(log_a) * (a * a + 1.0)) * i * xc
        return carry

    lax.fori_loop(0, n_chunks, gates, 0)

    def seg_rows(g, j):
        return pl.ds(pl.multiple_of(j * ns + g * SUBLANES, SUBLANES), SUBLANES)

    def local(j, carry):
        out = []
        for d in range(2):
            jj = j if d == 0 else steps - 1 - j
            for g in range(chains):
                h, p = carry[d * chains + g]
                a = a_s[d, seg_rows(g, jj), :]
                out.append((a * h + u_s[d, seg_rows(g, jj), :], a * p))
        return tuple(out)

    zero = jnp.zeros((SUBLANES, LANES), F32)
    one = jnp.ones((SUBLANES, LANES), F32)
    ends = lax.fori_loop(0, steps, local, tuple((zero, one) for _ in range(2 * chains)), unroll=8)

    starts = []
    for d in range(2):
        h_end = jnp.concatenate([ends[d * chains + g][0] for g in range(chains)], axis=0)
        p_end = jnp.concatenate([ends[d * chains + g][1] for g in range(chains)], axis=0)
        order = range(ns) if d == 0 else range(ns - 1, -1, -1)
        carry = jnp.zeros((1, LANES), F32)
        rows = [None] * ns
        for s in order:
            rows[s] = carry
            carry = p_end[s:s + 1, :] * carry + h_end[s:s + 1, :]
        init = jnp.concatenate(rows, axis=0)
        starts.extend(init[g * SUBLANES:(g + 1) * SUBLANES] for g in range(chains))

    def final(j, carry):
        out = []
        for d in range(2):
            jj = j if d == 0 else steps - 1 - j
            for g in range(chains):
                h = a_s[d, seg_rows(g, jj), :] * carry[d * chains + g] + u_s[d, seg_rows(g, jj), :]
                u_s[d, seg_rows(g, jj), :] = h
                out.append(h)
        return tuple(out)

    lax.fori_loop(0, steps, final, tuple(starts), unroll=8)

    def combine(c, carry):
        t0 = pl.multiple_of(c * t, t)
        h = u_s[0, pl.ds(t0, t), :] + u_s[1, pl.ds(t0, t), :]
        y_ref[pl.ds(t0, t), :] = _gelu_tanh(xg_ref[pl.ds(t0, t), :].astype(F32)) * h
        return carry

    lax.fori_loop(0, n_chunks, combine, 0)


def _rglru(xr, xg, conv_w, conv_b, w_a, b_a, w_i, b_i, lam, batch, seq):
    m, d_rnn = xr.shape
    blk = RNN_BLOCK
    assert seq % RNN_CHUNK == 0 and RNN_CHUNK % RNN_SEGMENTS == 0 and RNN_SEGMENTS % SUBLANES == 0
    slab = pl.BlockSpec((seq, blk), lambda b, n: (b, n))
    vec2 = pl.BlockSpec((2, blk), lambda b, n: (0, n))
    wspec = pl.BlockSpec((2, None, blk, blk), lambda b, n: (0, n, 0, 0))
    return pl.pallas_call(
        _rglru_kernel,
        out_shape=jax.ShapeDtypeStruct((m, d_rnn), F32),
        grid=(batch, d_rnn // blk),
        in_specs=[
            slab, slab,
            pl.BlockSpec((CONV_W, blk), lambda b, n: (0, n)),
            pl.BlockSpec((1, blk), lambda b, n: (0, n)),
            wspec, vec2, wspec, vec2, vec2,
        ],
        out_specs=slab,
        scratch_shapes=[pltpu.VMEM((seq + (CONV_W - 1) * RNN_SEGMENTS, blk), F32),
                        pltpu.VMEM((2, seq, blk), F32), pltpu.VMEM((2, seq, blk), F32)],
        compiler_params=_params(("arbitrary", "arbitrary")),
        name="rg_lru",
    )(xr, xg, conv_w, conv_b.reshape(1, d_rnn), w_a, b_a, w_i, b_i, lam)


def _rms_rows_kernel(y_ref, g_ref, o_ref):
    y = y_ref[...]
    r = lax.rsqrt(jnp.mean(y * y, axis=-1, keepdims=True) + RMS_EPS)
    o_ref[...] = (y * r * g_ref[...]).astype(BF16)


def _rms_rows(y, gain):
    m, d = y.shape
    return pl.pallas_call(
        _rms_rows_kernel,
        out_shape=jax.ShapeDtypeStruct((m, d), BF16),
        grid=(m // ROW_TM,),
        in_specs=[pl.BlockSpec((ROW_TM, d), lambda i: (i, 0)), pl.BlockSpec((1, d), lambda i: (0, 0))],
        out_specs=pl.BlockSpec((ROW_TM, d), lambda i: (i, 0)),
        compiler_params=_params(("arbitrary",)),
        name="rnn_rms",
    )(y, gain)


def _outproj_kernel(a1_ref, a2_ref, w_ref, o_ref):
    k1 = a1_ref.shape[1]
    o_ref[...] = (jnp.dot(a1_ref[...], w_ref[0:k1, :], preferred_element_type=F32)
                  + jnp.dot(a2_ref[...], w_ref[k1:, :], preferred_element_type=F32)).astype(o_ref.dtype)


def _outproj(a1, a2, w_o):
    m, k1 = a1.shape
    k2 = a2.shape[1]
    n = w_o.shape[1]
    return pl.pallas_call(
        _outproj_kernel,
        out_shape=jax.ShapeDtypeStruct((m, n), BF16),
        grid=(m // MM_TM, n // MM_TN),
        in_specs=[
            pl.BlockSpec((MM_TM, k1), lambda i, j: (i, 0)),
            pl.BlockSpec((MM_TM, k2), lambda i, j: (i, 0)),
            pl.BlockSpec((k1 + k2, MM_TN), lambda i, j: (0, j)),
        ],
        out_specs=pl.BlockSpec((MM_TM, MM_TN), lambda i, j: (i, j)),
        compiler_params=_params(("arbitrary", "arbitrary")),
        name="out_proj",
    )(a1, a2, w_o)


def _deepnorm_kernel(x_ref, y_ref, gate_ref, g_ref, b_ref, *rest, alpha, modulate):
    s = alpha * x_ref[...] + gate_ref[...] * y_ref[...].astype(F32)
    mu = jnp.mean(s, axis=-1, keepdims=True)
    sc = s - mu
    var = jnp.mean(sc * sc, axis=-1, keepdims=True)
    out = sc * lax.rsqrt(var + LN_EPS) * g_ref[...] + b_ref[...]
    if modulate:
        sc_ref, sh_ref, o_ref, u_ref = rest
        u_ref[...] = (out * (1.0 + sc_ref[...]) + sh_ref[...]).astype(BF16)
        o_ref[...] = out
    else:
        (o_ref,) = rest
        ns = o_ref.shape[0]
        for jj in range(SUBLANES):
            o_ref[:, jj, :] = out[jj * ns:(jj + 1) * ns, :]


def _deepnorm(x2, y, gate, ln_g, ln_b, nxt, alpha, seq):
    m, d = x2.shape
    per_b = seq // ROW_TM
    rows = pl.BlockSpec((ROW_TM, d), lambda i: (i, 0))
    vec_b = pl.BlockSpec((None, 1, d), lambda i: (i // per_b, 0, 0))
    vec = pl.BlockSpec((1, d), lambda i: (0, 0))
    modulate = nxt is not None
    in_specs = [rows, rows, vec_b, vec, vec]
    args = [x2, y, gate, ln_g.reshape(1, d), ln_b.reshape(1, d)]
    if modulate:
        in_specs += [vec_b, vec_b]
        args += list(nxt)
        out_shape = [jax.ShapeDtypeStruct((m, d), F32), jax.ShapeDtypeStruct((m, d), BF16)]
        out_specs = [rows, rows]
    else:
        ns = RNN_SEGMENTS
        assert ROW_TM == SUBLANES * ns
        out_shape = [jax.ShapeDtypeStruct((m // seq, ns, seq // ns, d), F32)]
        out_specs = [pl.BlockSpec((None, ns, SUBLANES, d), lambda i: (i // per_b, 0, i % per_b, 0))]
    res = pl.pallas_call(
        functools.partial(_deepnorm_kernel, alpha=alpha, modulate=modulate),
        out_shape=tuple(out_shape),
        grid=(m // ROW_TM,),
        in_specs=in_specs,
        out_specs=tuple(out_specs),
        compiler_params=_params(("arbitrary",)),
        name="deepnorm",
    )(*args)
    return res if modulate else (res[0], None)


def _ffn_in_kernel(u_ref, wg_ref, wu_ref, h_ref):
    u = u_ref[...]
    gate = jnp.dot(u, wg_ref[...], preferred_element_type=F32)
    up = jnp.dot(u, wu_ref[...], preferred_element_type=F32)
    h_ref[...] = (gate * jax.nn.sigmoid(gate) * up).astype(BF16)


def _ffn_in(u, w_gate, w_up):
    m, d = u.shape
    d_ff = w_gate.shape[1]
    return pl.pallas_call(
        _ffn_in_kernel,
        out_shape=jax.ShapeDtypeStruct((m, d_ff), BF16),
        grid=(m // MM_TM, pl.cdiv(d_ff, FFN_TN)),
        in_specs=[
            pl.BlockSpec((MM_TM, d), lambda i, j: (i, 0)),
            pl.BlockSpec((d, FFN_TN), lambda i, j: (0, j)),
            pl.BlockSpec((d, FFN_TN), lambda i, j: (0, j)),
        ],
        out_specs=pl.BlockSpec((MM_TM, FFN_TN), lambda i, j: (i, j)),
        compiler_params=_params(("arbitrary", "arbitrary")),
        name="ffn_in",
    )(u, w_gate, w_up)


def _ffn_down_kernel(h_ref, w_ref, o_ref):
    o_ref[...] = jnp.dot(h_ref[...], w_ref[...], preferred_element_type=F32).astype(o_ref.dtype)


def _ffn_down(h, w_down):
    m, kdim = h.shape
    n = w_down.shape[1]
    return pl.pallas_call(
        _ffn_down_kernel,
        out_shape=jax.ShapeDtypeStruct((m, n), BF16),
        grid=(n // DOWN_TN, m // DOWN_TM),
        in_specs=[
            pl.BlockSpec((DOWN_TM, kdim), lambda j, i: (i, 0)),
            pl.BlockSpec((kdim, DOWN_TN), lambda j, i: (0, j), pipeline_mode=pl.Buffered(1)),
        ],
        out_specs=pl.BlockSpec((DOWN_TM, DOWN_TN), lambda j, i: (i, j)),
        compiler_params=_params(("arbitrary", "arbitrary")),
        name="ffn_down",
    )(h, w_down)


def _rope_tables(seq):
    pos = jnp.arange(seq, dtype=jnp.int32)
    row = (pos // GRID_W).astype(F32)
    col = (pos % GRID_W).astype(F32)
    inv_freq = jnp.power(ROPE_THETA, -jnp.arange(0, ROPE_AXIS_DIM, 2, dtype=F32) / ROPE_AXIS_DIM)
    ang_r = row[:, None] * inv_freq[None, :]
    ang_c = col[:, None] * inv_freq[None, :]
    cos_r, sin_r, cos_c, sin_c = jnp.cos(ang_r), jnp.sin(ang_r), jnp.cos(ang_c), jnp.sin(ang_c)
    zero = jnp.zeros_like(sin_r)
    cos = jnp.concatenate([cos_r, cos_r, cos_c, cos_c], axis=-1)
    sin_a = jnp.concatenate([-sin_r, zero, -sin_c, zero], axis=-1)
    sin_b = jnp.concatenate([zero, sin_r, zero, sin_c], axis=-1)
    return cos, sin_a, sin_b


def kernel(x, c, w_ada, b_ada, w_in, qk_norm_g, conv_w, conv_b, rg_w_a, rg_b_a, rg_w_i, rg_b_i,
           rg_lam, mix_out_g, w_o, ln_g, ln_b, w_ffn_in, w_down):
    batch, seq, d = x.shape
    depth = w_ada.shape[0]
    m = batch * seq
    alpha = (2 * depth) ** 0.25
    assert d == D_MODEL and seq % ATTN_TK == 0 and seq % INPROJ_TM == 0 and m % MM_TM == 0
    assert m % DOWN_TM == 0 and d % DOWN_TN == 0

    ns = RNN_SEGMENTS

    def interleave(a):
        lead = a.shape[:-2]
        return a.reshape(*lead, ns, seq // ns, a.shape[-1]).swapaxes(-3, -2).reshape(a.shape)

    cos, sin_a, sin_b = (interleave(tab) for tab in _rope_tables(seq))
    c_pad = jnp.pad(c, ((0, SUBLANES - batch), (0, 0)))
    mod = _ada(c_pad, w_ada, b_ada)[:, :batch].reshape(depth, batch, 6, 1, d)

    w_in_b = w_in.astype(BF16)
    w_o_b = w_o.astype(BF16)
    w_gate_b = w_ffn_in[:, :, :D_FF].astype(BF16)
    w_up_b = w_ffn_in[:, :, D_FF:].astype(BF16)
    w_down_b = w_down.astype(BF16)

    x2 = interleave(x).reshape(m, d)
    u = _modulate(x2, mod[0, :, 1], mod[0, :, 0], seq)
    for l in range(depth):
        sh2, sc2, g1, g2 = mod[l, :, 3], mod[l, :, 4], mod[l, :, 2], mod[l, :, 5]
        q, k, vt, xr, xg = _inproj(u, w_in_b[l], qk_norm_g[l], cos, sin_a, sin_b, seq)
        y_attn = _attention(q, k, vt, mix_out_g[l, :D_ATTN].reshape(1, D_ATTN), batch, seq)
        y_rnn = _rglru(xr, xg, conv_w[l], conv_b[l], rg_w_a[l], rg_b_a[l], rg_w_i[l], rg_b_i[l],
                       rg_lam[l], batch, seq)
        y_rnn = _rms_rows(y_rnn, mix_out_g[l, D_ATTN:].reshape(1, D_RNN))
        y = _outproj(y_attn, y_rnn, w_o_b[l])
        x2, u = _deepnorm(x2, y, g1, ln_g[l, 0], ln_b[l, 0], (sc2, sh2), alpha, seq)
        h = _ffn_in(u, w_gate_b[l], w_up_b[l])
        y = _ffn_down(h, w_down_b[l])
        nxt = (mod[l + 1, :, 1], mod[l + 1, :, 0]) if l + 1 < depth else None
        x2, u = _deepnorm(x2, y, g2, ln_g[l, 1], ln_b[l, 1], nxt, alpha, seq)
    return x2.reshape(batch, seq, d)
```

```python
import functools
import math

import jax
import jax.numpy as jnp
from jax import lax
from jax.experimental import pallas as pl
from jax.experimental.pallas import tpu as pltpu

D_MODEL = 4096
GRID_W = 64
HEAD_DIM = 128
D_ATTN = D_MODEL // 2
N_Q_HEADS = D_ATTN // HEAD_DIM
N_KV_HEADS = 4
Q_PER_KV = N_Q_HEADS // N_KV_HEADS
D_KV = N_KV_HEADS * HEAD_DIM
D_RNN = D_MODEL - D_ATTN
RNN_BLOCK = 128
N_RNN_BLOCKS = D_RNN // RNN_BLOCK
D_IN = D_ATTN + 2 * D_KV + 2 * D_RNN
CONV_W = 4
RG_C = 8.0
ROPE_THETA = 10000.0
ROPE_AXIS_DIM = HEAD_DIM // 2
D_FF = (-(-8 * D_MODEL // (3 * 256))) * 256
LN_EPS = 1e-5
RMS_EPS = 1e-6
Q_SCALE = HEAD_DIM ** -0.5 * math.log2(math.e)

V7X_VMEM_BYTES = 64 * 1024 * 1024
VMEM_LIMIT = V7X_VMEM_BYTES - 8 * 1024 * 1024
LANES = 128
SUBLANES = 8

BF16 = jnp.bfloat16
F32 = jnp.float32

ADA_TN = 1024
ADA_K_SPLIT = 4
INPROJ_TM = 1024
INPROJ_TN = 512
ATTN_TQ = 512
ATTN_TK = 1024
ATTN_SCORE_SPLIT = 4
ATTN_LOOKAHEAD = 2
VT_ROWS = HEAD_DIM + 16
MM_TM = 1024
MM_TN = 1024
FFN_TN = 512
FFN_UP_TN = 256
DOWN_TM = 512
DOWN_TN = 1024
RNN_SEGMENTS = 32
ROW_TM = 256
RNN_CHUNK = 512


def _params(sem):
    return pltpu.CompilerParams(dimension_semantics=sem, vmem_limit_bytes=VMEM_LIMIT)


def _ada_kernel(c_ref, *refs):
    w_refs, (b_ref, o_ref) = refs[:ADA_K_SPLIT], refs[ADA_K_SPLIT:]
    c = c_ref[...]
    c_act = (c * jax.nn.sigmoid(c)).astype(BF16)
    kc = w_refs[0].shape[0]
    acc = b_ref[...]
    for s, w_ref in enumerate(w_refs):
        acc = acc + jnp.dot(c_act[:, s * kc:(s + 1) * kc], w_ref[...].astype(BF16),
                            preferred_element_type=F32)
    o_ref[...] = acc


def _ada(c_pad, w_ada, b_ada):
    depth, d, n = w_ada.shape
    kc = d // ADA_K_SPLIT
    w_specs = [pl.BlockSpec((None, kc, ADA_TN), functools.partial(lambda l, j, s: (l, s, j), s=s))
               for s in range(ADA_K_SPLIT)]
    return pl.pallas_call(
        _ada_kernel,
        out_shape=jax.ShapeDtypeStruct((depth, SUBLANES, n), F32),
        grid=(depth, n // ADA_TN),
        in_specs=[pl.BlockSpec((SUBLANES, d), lambda l, j: (0, 0))] + w_specs
        + [pl.BlockSpec((None, 1, ADA_TN), lambda l, j: (l, 0, j))],
        out_specs=pl.BlockSpec((None, SUBLANES, ADA_TN), lambda l, j: (l, 0, j)),
        compiler_params=_params(("arbitrary", "arbitrary")),
        name="ada_mod",
    )(c_pad, *([w_ada] * ADA_K_SPLIT), b_ada.reshape(depth, 1, n))


def _modulate_kernel(x_ref, sc_ref, sh_ref, u_ref):
    u_ref[...] = (x_ref[...] * (1.0 + sc_ref[...]) + sh_ref[...]).astype(BF16)


def _modulate(x2, sc, sh, seq):
    m, d = x2.shape
    per_b = seq // ROW_TM
    vec = pl.BlockSpec((None, 1, d), lambda i: (i // per_b, 0, 0))
    return pl.pallas_call(
        _modulate_kernel,
        out_shape=jax.ShapeDtypeStruct((m, d), BF16),
        grid=(m // ROW_TM,),
        in_specs=[pl.BlockSpec((ROW_TM, d), lambda i: (i, 0)), vec, vec],
        out_specs=pl.BlockSpec((ROW_TM, d), lambda i: (i, 0)),
        compiler_params=_params(("arbitrary",)),
        name="modulate",
    )(x2, sc, sh)


def _rms_rope(xh, gain, cos, sin_a, sin_b):
    r = lax.rsqrt(jnp.mean(xh * xh, axis=-1, keepdims=True) + RMS_EPS)
    xn = xh * r * gain
    return xn * cos + pltpu.roll(xn, 96, 1) * sin_a + pltpu.roll(xn, 32, 1) * sin_b


_Q_TILES = D_ATTN // INPROJ_TN
_K_TILE = _Q_TILES
_V_TILE = _K_TILE + 1
_R_TILE0 = _V_TILE + 1
_R_TILES = D_RNN // INPROJ_TN
_G_TILE0 = _R_TILE0 + _R_TILES
assert D_KV == INPROJ_TN and D_IN == (_G_TILE0 + _R_TILES) * INPROJ_TN


def _inproj_kernel(u_ref, w_ref, g_ref, cos_ref, sa_ref, sb_ref,
                   q_ref, k_ref, vt_ref, xr_ref, xg_ref):
    j = pl.program_id(1)
    z = jnp.dot(u_ref[...], w_ref[...], preferred_element_type=F32)
    heads = INPROJ_TN // HEAD_DIM

    def head(h):
        return z[:, h * HEAD_DIM:(h + 1) * HEAD_DIM]

    @pl.when(j < _Q_TILES)
    def _():
        cos, sa, sb = cos_ref[...], sa_ref[...], sb_ref[...]
        for h in range(heads):
            q_ref[h] = (_rms_rope(head(h), g_ref[0:1, :], cos, sa, sb) * Q_SCALE).astype(BF16)

    @pl.when(j == _K_TILE)
    def _():
        cos, sa, sb = cos_ref[...], sa_ref[...], sb_ref[...]
        for h in range(heads):
            k_ref[h] = _rms_rope(head(h), g_ref[1:2, :], cos, sa, sb).astype(BF16)

    @pl.when(j == _V_TILE)
    def _():
        for h in range(heads):
            vt_ref[h, 0:HEAD_DIM, :] = head(h).T.astype(BF16)
            vt_ref[h, HEAD_DIM:, :] = jnp.ones((VT_ROWS - HEAD_DIM, z.shape[0]), BF16)

    @pl.when((j >= _R_TILE0) & (j < _G_TILE0))
    def _():
        xr_ref[...] = z.astype(BF16)

    @pl.when(j >= _G_TILE0)
    def _():
        xg_ref[...] = z.astype(BF16)


def _inproj(u, w_in, qk_g, cos, sin_a, sin_b, seq):
    m, d = u.shape
    tm, tn = INPROJ_TM, INPROJ_TN
    per_b = seq // tm
    heads = tn // HEAD_DIM
    tab = pl.BlockSpec((tm, HEAD_DIM), lambda i, j: (i % per_b, 0))
    return pl.pallas_call(
        _inproj_kernel,
        out_shape=(
            jax.ShapeDtypeStruct((N_Q_HEADS, m, HEAD_DIM), BF16),
            jax.ShapeDtypeStruct((N_KV_HEADS, m, HEAD_DIM), BF16),
            jax.ShapeDtypeStruct((N_KV_HEADS, VT_ROWS, m), BF16),
            jax.ShapeDtypeStruct((m, D_RNN), BF16),
            jax.ShapeDtypeStruct((m, D_RNN), BF16),
        ),
        grid=(m // tm, D_IN // tn),
        in_specs=[
            pl.BlockSpec((tm, d), lambda i, j: (i, 0)),
            pl.BlockSpec((d, tn), lambda i, j: (0, j)),
            pl.BlockSpec((2, HEAD_DIM), lambda i, j: (0, 0)),
            tab, tab, tab,
        ],
        out_specs=(
            pl.BlockSpec((heads, tm, HEAD_DIM), lambda i, j: (jnp.minimum(j, _Q_TILES - 1), i, 0)),
            pl.BlockSpec((N_KV_HEADS, tm, HEAD_DIM), lambda i, j: (0, i, 0)),
            pl.BlockSpec((N_KV_HEADS, VT_ROWS, tm), lambda i, j: (0, 0, i)),
            pl.BlockSpec((tm, tn), lambda i, j: (i, jnp.clip(j - _R_TILE0, 0, _R_TILES - 1))),
            pl.BlockSpec((tm, tn), lambda i, j: (i, jnp.clip(j - _G_TILE0, 0, _R_TILES - 1))),
        ),
        compiler_params=_params(("arbitrary", "arbitrary")),
        name="in_proj",
    )(u, w_in, qk_g, cos, sin_a, sin_b)


def _attn_kernel(q_ref, k_ref, vt_ref, g_ref, o_ref, m_s, acc_s):
    j = pl.program_id(2)
    tq = q_ref.shape[1]
    tk = k_ref.shape[1]
    rows = tk // ATTN_SCORE_SPLIT

    @pl.when(j == 0)
    def _():
        m_s[...] = jnp.full(m_s.shape, -jnp.inf, F32)
        acc_s[...] = jnp.zeros(acc_s.shape, F32)

    def scores(kh, g):
        q = q_ref[Q_PER_KV * kh + g]
        return [lax.dot_general(k_ref[kh, r * rows:(r + 1) * rows, :], q, (((1,), (1,)), ((), ())),
                                preferred_element_type=F32) for r in range(ATTN_SCORE_SPLIT)]

    def update(kh, g, s_parts):
        cols = slice(g * tq, (g + 1) * tq)
        m_old = m_s[kh:kh + 1, cols]
        m_new = m_old
        for s_t in s_parts:
            m_new = jnp.maximum(m_new, jnp.max(s_t, axis=0, keepdims=True))
        alpha = jnp.exp2(m_old - m_new)
        p = jnp.concatenate([jnp.exp2(s_t - m_new).astype(BF16) for s_t in s_parts], axis=0)
        pv = jnp.dot(vt_ref[kh], p, preferred_element_type=F32)
        acc_s[kh, :, cols] = alpha * acc_s[kh, :, cols] + pv
        m_s[kh:kh + 1, cols] = m_new

    blocks = [(kh, g) for kh in range(N_KV_HEADS) for g in range(Q_PER_KV)]
    pending = [scores(*blk) for blk in blocks[:ATTN_LOOKAHEAD]]
    for c, (kh, g) in enumerate(blocks):
        if c + ATTN_LOOKAHEAD < len(blocks):
            pending.append(scores(*blocks[c + ATTN_LOOKAHEAD]))
        update(kh, g, pending.pop(0))

    @pl.when(j == pl.num_programs(2) - 1)
    def _():
        outs = []
        ss = jnp.zeros((1, tq), F32)
        for kh in range(N_KV_HEADS):
            o = acc_s[kh, 0:HEAD_DIM, :] / acc_s[kh, HEAD_DIM:HEAD_DIM + 1, :]
            sq = jnp.sum(o * o, axis=0, keepdims=True)
            for g in range(Q_PER_KV):
                ss = ss + sq[:, g * tq:(g + 1) * tq]
            outs.append(o)
        r = lax.rsqrt(ss * (1.0 / D_ATTN) + RMS_EPS)
        for kh in range(N_KV_HEADS):
            for g in range(Q_PER_KV):
                h = kh * Q_PER_KV + g
                oh = (outs[kh][:, g * tq:(g + 1) * tq] * r).T
                o_ref[:, h * HEAD_DIM:(h + 1) * HEAD_DIM] = (
                    oh * g_ref[:, h * HEAD_DIM:(h + 1) * HEAD_DIM]).astype(BF16)


def _attention(q, k, vt, out_g_attn, batch, seq):
    tq, tk = ATTN_TQ, ATTN_TK
    nq, nk = seq // tq, seq // tk
    m = batch * seq
    return pl.pallas_call(
        _attn_kernel,
        out_shape=jax.ShapeDtypeStruct((m, D_ATTN), BF16),
        grid=(batch, nq, nk),
        in_specs=[
            pl.BlockSpec((N_Q_HEADS, tq, HEAD_DIM), lambda b, i, j: (0, b * nq + i, 0)),
            pl.BlockSpec((N_KV_HEADS, tk, HEAD_DIM), lambda b, i, j: (0, b * nk + j, 0)),
            pl.BlockSpec((N_KV_HEADS, VT_ROWS, tk), lambda b, i, j: (0, 0, b * nk + j)),
            pl.BlockSpec((1, D_ATTN), lambda b, i, j: (0, 0)),
        ],
        out_specs=pl.BlockSpec((tq, D_ATTN), lambda b, i, j: (b * nq + i, 0)),
        scratch_shapes=[
            pltpu.VMEM((SUBLANES, Q_PER_KV * tq), F32),
            pltpu.VMEM((N_KV_HEADS, VT_ROWS, Q_PER_KV * tq), F32),
        ],
        compiler_params=_params(("arbitrary", "arbitrary", "arbitrary")),
        name="attention",
    )(q, k, vt, out_g_attn)


def _gelu_tanh(x):
    return 0.5 * x * (1.0 + jnp.tanh(math.sqrt(2.0 / math.pi) * (x + 0.044715 * (x * x * x))))


def _softplus(x):
    return jnp.maximum(x, 0.0) + jnp.log1p(jnp.exp(-jnp.abs(x)))


def _rglru_kernel(xr_ref, xg_ref, cw_ref, cb_ref, wa_ref, ba_ref, wi_ref, bi_ref, lam_ref,
                  y_ref, xe_s, a_s, u_s):
    seq = xr_ref.shape[0]
    ns = RNN_SEGMENTS
    steps = seq // ns
    t = RNN_CHUNK
    n_chunks = seq // t
    chains = ns // SUBLANES

    cw = cw_ref[...]
    cb = cb_ref[...]
    coef = -RG_C * _softplus(-lam_ref[...])
    w_a = [wa_ref[d].astype(BF16) for d in range(2)]
    w_i = [wi_ref[d].astype(BF16) for d in range(2)]

    seg_id = lax.broadcasted_iota(jnp.int32, (ns, LANES), 0)
    xe_s[pl.ds(0, ns), :] = jnp.where(
        seg_id == 0, 0.0, pltpu.roll(xr_ref[pl.ds(seq - ns, ns), :].astype(F32), 1, 0))
    for k in range(2):
        xe_s[pl.ds(ns + seq + k * ns, ns), :] = jnp.where(
            seg_id == ns - 1, 0.0, pltpu.roll(xr_ref[pl.ds(k * ns, ns), :].astype(F32), ns - 1, 0))

    def stage(c, carry):
        t0 = pl.multiple_of(c * t, t)
        xe_s[pl.ds(ns + t0, t), :] = xr_ref[pl.ds(t0, t), :].astype(F32)
        return carry

    lax.fori_loop(0, n_chunks, stage, 0)

    def gates(c, carry):
        t0 = pl.multiple_of(c * t, t)
        xc = cb
        for tap in range(CONV_W):
            xc = xc + cw[tap:tap + 1, :] * xe_s[pl.ds(t0 + tap * ns, t), :]
        xcb = xc.astype(BF16)
        for d in range(2):
            r = jax.nn.sigmoid(jnp.dot(xcb, w_a[d], preferred_element_type=F32) + ba_ref[d:d + 1, :])
            i = jax.nn.sigmoid(jnp.dot(xcb, w_i[d], preferred_element_type=F32) + bi_ref[d:d + 1, :])
            log_a = coef[d:d + 1, :] * r
            a = jnp.exp(log_a)
            a_s[d, pl.ds(t0, t), :] = a
            u_s[d, pl.ds(t0, t), :] = jnp.sqrt(-jnp.tanh(log_a) * (a * a + 1.0)) * i * xc
        return carry

    lax.fori_loop(0, n_chunks, gates, 0)

    def seg_rows(g, j):
        return pl.ds(pl.multiple_of(j * ns + g * SUBLANES, SUBLANES), SUBLANES)

    def local(j, carry):
        out = []
        for d in range(2):
            jj = j if d == 0 else steps - 1 - j
            for g in range(chains):
                h, p = carry[d * chains + g]
                a = a_s[d, seg_rows(g, jj), :]
                out.append((a * h + u_s[d, seg_rows(g, jj), :], a * p))
        return tuple(out)

    zero = jnp.zeros((SUBLANES, LANES), F32)
    one = jnp.ones((SUBLANES, LANES), F32)
    ends = lax.fori_loop(0, steps, local, tuple((zero, one) for _ in range(2 * chains)), unroll=8)

    starts = []
    for d in range(2):
        h_end = jnp.concatenate([ends[d * chains + g][0] for g in range(chains)], axis=0)
        p_end = jnp.concatenate([ends[d * chains + g][1] for g in range(chains)], axis=0)
        order = range(ns) if d == 0 else range(ns - 1, -1, -1)
        carry = jnp.zeros((1, LANES), F32)
        rows = [None] * ns
        for s in order:
            rows[s] = carry
            carry = p_end[s:s + 1, :] * carry + h_end[s:s + 1, :]
        init = jnp.concatenate(rows, axis=0)
        starts.extend(init[g * SUBLANES:(g + 1) * SUBLANES] for g in range(chains))

    def final(j, carry):
        out = []
        for d in range(2):
            jj = j if d == 0 else steps - 1 - j
            for g in range(chains):
                h = a_s[d, seg_rows(g, jj), :] * carry[d * chains + g] + u_s[d, seg_rows(g, jj), :]
                u_s[d, seg_rows(g, jj), :] = h
                out.append(h)
        return tuple(out)

    lax.fori_loop(0, steps, final, tuple(starts), unroll=8)

    def combine(c, carry):
        t0 = pl.multiple_of(c * t, t)
        h = u_s[0, pl.ds(t0, t), :] + u_s[1, pl.ds(t0, t), :]
        y_ref[pl.ds(t0, t), :] = _gelu_tanh(xg_ref[pl.ds(t0, t), :].astype(F32)) * h
        return carry

    lax.fori_loop(0, n_chunks, combine, 0)


def _rglru(xr, xg, conv_w, conv_b, w_a, b_a, w_i, b_i, lam, batch, seq):
    m, d_rnn = xr.shape
    blk = RNN_BLOCK
    assert seq % RNN_CHUNK == 0 and RNN_CHUNK % RNN_SEGMENTS == 0 and RNN_SEGMENTS % SUBLANES == 0
    slab = pl.BlockSpec((seq, blk), lambda b, n: (b, n))
    vec2 = pl.BlockSpec((2, blk), lambda b, n: (0, n))
    wspec = pl.BlockSpec((2, None, blk, blk), lambda b, n: (0, n, 0, 0))
    return pl.pallas_call(
        _rglru_kernel,
        out_shape=jax.ShapeDtypeStruct((m, d_rnn), F32),
        grid=(batch, d_rnn // blk),
        in_specs=[
            slab, slab,
            pl.BlockSpec((CONV_W, blk), lambda b, n: (0, n)),
            pl.BlockSpec((1, blk), lambda b, n: (0, n)),
            wspec, vec2, wspec, vec2, vec2,
        ],
        out_specs=slab,
        scratch_shapes=[pltpu.VMEM((seq + (CONV_W - 1) * RNN_SEGMENTS, blk), F32),
                        pltpu.VMEM((2, seq, blk), F32), pltpu.VMEM((2, seq, blk), F32)],
        compiler_params=_params(("arbitrary", "arbitrary")),
        name="rg_lru",
    )(xr, xg, conv_w, conv_b.reshape(1, d_rnn), w_a, b_a, w_i, b_i, lam)


def _rms_rows_kernel(y_ref, g_ref, o_ref):
    y = y_ref[...]
    r = lax.rsqrt(jnp.mean(y * y, axis=-1, keepdims=True) + RMS_EPS)
    o_ref[...] = (y * r * g_ref[...]).astype(BF16)


def _rms_rows(y, gain):
    m, d = y.shape
    return pl.pallas_call(
        _rms_rows_kernel,
        out_shape=jax.ShapeDtypeStruct((m, d), BF16),
        grid=(m // ROW_TM,),
        in_specs=[pl.BlockSpec((ROW_TM, d), lambda i: (i, 0)), pl.BlockSpec((1, d), lambda i: (0, 0))],
        out_specs=pl.BlockSpec((ROW_TM, d), lambda i: (i, 0)),
        compiler_params=_params(("arbitrary",)),
        name="rnn_rms",
    )(y, gain)


def _outproj_kernel(a1_ref, a2_ref, w_ref, o_ref):
    k1 = a1_ref.shape[1]
    o_ref[...] = (jnp.dot(a1_ref[...], w_ref[0:k1, :], preferred_element_type=F32)
                  + jnp.dot(a2_ref[...], w_ref[k1:, :], preferred_element_type=F32)).astype(o_ref.dtype)


def _outproj(a1, a2, w_o):
    m, k1 = a1.shape
    k2 = a2.shape[1]
    n = w_o.shape[1]
    return pl.pallas_call(
        _outproj_kernel,
        out_shape=jax.ShapeDtypeStruct((m, n), BF16),
        grid=(m // MM_TM, n // MM_TN),
        in_specs=[
            pl.BlockSpec((MM_TM, k1), lambda i, j: (i, 0)),
            pl.BlockSpec((MM_TM, k2), lambda i, j: (i, 0)),
            pl.BlockSpec((k1 + k2, MM_TN), lambda i, j: (0, j)),
        ],
        out_specs=pl.BlockSpec((MM_TM, MM_TN), lambda i, j: (i, j)),
        compiler_params=_params(("arbitrary", "arbitrary")),
        name="out_proj",
    )(a1, a2, w_o)


def _deepnorm_kernel(x_ref, y_ref, gate_ref, g_ref, b_ref, *rest, alpha, modulate):
    s = alpha * x_ref[...] + gate_ref[...] * y_ref[...].astype(F32)
    mu = jnp.mean(s, axis=-1, keepdims=True)
    sc = s - mu
    var = jnp.mean(sc * sc, axis=-1, keepdims=True)
    out = sc * lax.rsqrt(var + LN_EPS) * g_ref[...] + b_ref[...]
    if modulate:
        sc_ref, sh_ref, o_ref, u_ref = rest
        u_ref[...] = (out * (1.0 + sc_ref[...]) + sh_ref[...]).astype(BF16)
        o_ref[...] = out
    else:
        (o_ref,) = rest
        ns = o_ref.shape[0]
        for jj in range(SUBLANES):
            o_ref[:, jj, :] = out[jj * ns:(jj + 1) * ns, :]


def _deepnorm(x2, y, gate, ln_g, ln_b, nxt, alpha, seq):
    m, d = x2.shape
    per_b = seq // ROW_TM
    rows = pl.BlockSpec((ROW_TM, d), lambda i: (i, 0))
    vec_b = pl.BlockSpec((None, 1, d), lambda i: (i // per_b, 0, 0))
    vec = pl.BlockSpec((1, d), lambda i: (0, 0))
    modulate = nxt is not None
    in_specs = [rows, rows, vec_b, vec, vec]
    args = [x2, y, gate, ln_g.reshape(1, d), ln_b.reshape(1, d)]
    if modulate:
        in_specs += [vec_b, vec_b]
        args += list(nxt)
        out_shape = [jax.ShapeDtypeStruct((m, d), F32), jax.ShapeDtypeStruct((m, d), BF16)]
        out_specs = [rows, rows]
    else:
        ns = RNN_SEGMENTS
        assert ROW_TM == SUBLANES * ns
        out_shape = [jax.ShapeDtypeStruct((m // seq, ns, seq // ns, d), F32)]
        out_specs = [pl.BlockSpec((None, ns, SUBLANES, d), lambda i: (i // per_b, 0, i % per_b, 0))]
    res = pl.pallas_call(
        functools.partial(_deepnorm_kernel, alpha=alpha, modulate=modulate),
        out_shape=tuple(out_shape),
        grid=(m // ROW_TM,),
        in_specs=in_specs,
        out_specs=tuple(out_specs),
        compiler_params=_params(("arbitrary",)),
        name="deepnorm",
    )(*args)
    return res if modulate else (res[0], None)


def _ffn_in_kernel(u_ref, wg_ref, *refs):
    up_refs, h_ref = refs[:-1], refs[-1]
    u = u_ref[...]
    gate = jnp.dot(u, wg_ref[...], preferred_element_type=F32)
    up = jnp.concatenate([jnp.dot(u, w_ref[...], preferred_element_type=F32) for w_ref in up_refs],
                         axis=1)
    h_ref[...] = (gate * jax.nn.sigmoid(gate) * up).astype(BF16)


def _ffn_in(u, w_ffn, d_ff):
    m, d = u.shape
    assert w_ffn.shape[1] == 2 * d_ff and d_ff % FFN_UP_TN == 0 and FFN_TN % FFN_UP_TN == 0
    parts = FFN_TN // FFN_UP_TN
    up0 = d_ff // FFN_UP_TN
    last = 2 * up0 - 1
    up_specs = [pl.BlockSpec((d, FFN_UP_TN), functools.partial(
        lambda i, j, p: (0, jnp.minimum(up0 + parts * j + p, last)), p=p)) for p in range(parts)]
    return pl.pallas_call(
        _ffn_in_kernel,
        out_shape=jax.ShapeDtypeStruct((m, d_ff), BF16),
        grid=(m // MM_TM, pl.cdiv(d_ff, FFN_TN)),
        in_specs=[pl.BlockSpec((MM_TM, d), lambda i, j: (i, 0)),
                  pl.BlockSpec((d, FFN_TN), lambda i, j: (0, j))] + up_specs,
        out_specs=pl.BlockSpec((MM_TM, FFN_TN), lambda i, j: (i, j)),
        compiler_params=_params(("arbitrary", "arbitrary")),
        name="ffn_in",
    )(u, w_ffn, *([w_ffn] * parts))


def _ffn_down_kernel(h_ref, w_ref, o_ref):
    o_ref[...] = jnp.dot(h_ref[...], w_ref[...], preferred_element_type=F32).astype(o_ref.dtype)


def _ffn_down(h, w_down):
    m, kdim = h.shape
    n = w_down.shape[1]
    return pl.pallas_call(
        _ffn_down_kernel,
        out_shape=jax.ShapeDtypeStruct((m, n), BF16),
        grid=(n // DOWN_TN, m // DOWN_TM),
        in_specs=[
            pl.BlockSpec((DOWN_TM, kdim), lambda j, i: (i, 0)),
            pl.BlockSpec((kdim, DOWN_TN), lambda j, i: (0, j), pipeline_mode=pl.Buffered(1)),
        ],
        out_specs=pl.BlockSpec((DOWN_TM, DOWN_TN), lambda j, i: (i, j)),
        compiler_params=_params(("arbitrary", "arbitrary")),
        name="ffn_down",
    )(h, w_down)


def _rope_tables(seq):
    pos = jnp.arange(seq, dtype=jnp.int32)
    row = (pos // GRID_W).astype(F32)
    col = (pos % GRID_W).astype(F32)
    inv_freq = jnp.power(ROPE_THETA, -jnp.arange(0, ROPE_AXIS_DIM, 2, dtype=F32) / ROPE_AXIS_DIM)
    ang_r = row[:, None] * inv_freq[None, :]
    ang_c = col[:, None] * inv_freq[None, :]
    cos_r, sin_r, cos_c, sin_c = jnp.cos(ang_r), jnp.sin(ang_r), jnp.cos(ang_c), jnp.sin(ang_c)
    zero = jnp.zeros_like(sin_r)
    cos = jnp.concatenate([cos_r, cos_r, cos_c, cos_c], axis=-1)
    sin_a = jnp.concatenate([-sin_r, zero, -sin_c, zero], axis=-1)
    sin_b = jnp.concatenate([zero, sin_r, zero, sin_c], axis=-1)
    return cos, sin_a, sin_b


def kernel(x, c, w_ada, b_ada, w_in, qk_norm_g, conv_w, conv_b, rg_w_a, rg_b_a, rg_w_i, rg_b_i,
           rg_lam, mix_out_g, w_o, ln_g, ln_b, w_ffn_in, w_down):
    batch, seq, d = x.shape
    depth = w_ada.shape[0]
    m = batch * seq
    alpha = (2 * depth) ** 0.25
    assert d == D_MODEL and seq % ATTN_TK == 0 and seq % INPROJ_TM == 0 and m % MM_TM == 0
    assert m % DOWN_TM == 0 and d % DOWN_TN == 0

    ns = RNN_SEGMENTS

    def interleave(a):
        lead = a.shape[:-2]
        return a.reshape(*lead, ns, seq // ns, a.shape[-1]).swapaxes(-3, -2).reshape(a.shape)

    cos, sin_a, sin_b = (interleave(tab) for tab in _rope_tables(seq))
    c_pad = jnp.pad(c, ((0, SUBLANES - batch), (0, 0)))
    mod = _ada(c_pad, w_ada, b_ada)[:, :batch].reshape(depth, batch, 6, 1, d)

    w_in_b = w_in.astype(BF16)
    w_o_b = w_o.astype(BF16)
    w_ffn_b = w_ffn_in.astype(BF16)
    w_down_b = w_down.astype(BF16)

    x2 = interleave(x).reshape(m, d)
    u = _modulate(x2, mod[0, :, 1], mod[0, :, 0], seq)
    for l in range(depth):
        sh2, sc2, g1, g2 = mod[l, :, 3], mod[l, :, 4], mod[l, :, 2], mod[l, :, 5]
        q, k, vt, xr, xg = _inproj(u, w_in_b[l], qk_norm_g[l], cos, sin_a, sin_b, seq)
        y_attn = _attention(q, k, vt, mix_out_g[l, :D_ATTN].reshape(1, D_ATTN), batch, seq)
        y_rnn = _rglru(xr, xg, conv_w[l], conv_b[l], rg_w_a[l], rg_b_a[l], rg_w_i[l], rg_b_i[l],
                       rg_lam[l], batch, seq)
        y_rnn = _rms_rows(y_rnn, mix_out_g[l, D_ATTN:].reshape(1, D_RNN))
        y = _outproj(y_attn, y_rnn, w_o_b[l])
        x2, u = _deepnorm(x2, y, g1, ln_g[l, 0], ln_b[l, 0], (sc2, sh2), alpha, seq)
        h = _ffn_in(u, w_ffn_b[l], D_FF)
        y = _ffn_down(h, w_down_b[l])
        nxt = (mod[l + 1, :, 1], mod[l + 1, :, 0]) if l + 1 < depth else None
        x2, u = _deepnorm(x2, y, g2, ln_g[l, 1], ln_b[l, 1], nxt, alpha, seq)
    return x2.reshape(batch, seq, d)
```

```python
import functools
import math

import jax
import jax.numpy as jnp
from jax import lax
from jax.experimental import pallas as pl
from jax.experimental.pallas import tpu as pltpu

D_MODEL = 4096
GRID_W = 64
HEAD_DIM = 128
D_ATTN = D_MODEL // 2
N_Q_HEADS = D_ATTN // HEAD_DIM
N_KV_HEADS = 4
Q_PER_KV = N_Q_HEADS // N_KV_HEADS
D_KV = N_KV_HEADS * HEAD_DIM
D_RNN = D_MODEL - D_ATTN
RNN_BLOCK = 128
N_RNN_BLOCKS = D_RNN // RNN_BLOCK
D_IN = D_ATTN + 2 * D_KV + 2 * D_RNN
CONV_W = 4
RG_C = 8.0
ROPE_THETA = 10000.0
ROPE_AXIS_DIM = HEAD_DIM // 2
D_FF = (-(-8 * D_MODEL // (3 * 256))) * 256
LN_EPS = 1e-5
RMS_EPS = 1e-6
Q_SCALE = HEAD_DIM ** -0.5 * math.log2(math.e)

V7X_VMEM_BYTES = 64 * 1024 * 1024
VMEM_LIMIT = V7X_VMEM_BYTES - 8 * 1024 * 1024
LANES = 128
SUBLANES = 8

BF16 = jnp.bfloat16
F32 = jnp.float32

ADA_TN = 1024
ADA_K_SPLIT = 4
INPROJ_TM = 1024
INPROJ_TN = 512
ATTN_TQ = 512
ATTN_TK = 1024
ATTN_SCORE_SPLIT = 4
ATTN_LOOKAHEAD = 2
VT_ROWS = HEAD_DIM + 16
MM_TM = 1024
MM_TN = 1024
FFN_TN = 512
FFN_UP_TN = 256
DOWN_TM = 512
DOWN_TN = 1024
RNN_SEGMENTS = 32
ROW_TM = 256
RNN_CHUNK = 512


def _params(sem):
    return pltpu.CompilerParams(dimension_semantics=sem, vmem_limit_bytes=VMEM_LIMIT)


def _ada_kernel(c_ref, *refs):
    w_refs, (b_ref, o_ref) = refs[:ADA_K_SPLIT], refs[ADA_K_SPLIT:]
    c = c_ref[...]
    c_act = (c * jax.nn.sigmoid(c)).astype(BF16)
    kc = w_refs[0].shape[0]
    acc = b_ref[...]
    for s, w_ref in enumerate(w_refs):
        acc = acc + jnp.dot(c_act[:, s * kc:(s + 1) * kc], w_ref[...].astype(BF16),
                            preferred_element_type=F32)
    o_ref[...] = acc


def _ada(c_pad, w_ada, b_ada):
    depth, d, n = w_ada.shape
    kc = d // ADA_K_SPLIT
    w_specs = [pl.BlockSpec((None, kc, ADA_TN), functools.partial(lambda l, j, s: (l, s, j), s=s))
               for s in range(ADA_K_SPLIT)]
    return pl.pallas_call(
        _ada_kernel,
        out_shape=jax.ShapeDtypeStruct((depth, SUBLANES, n), F32),
        grid=(depth, n // ADA_TN),
        in_specs=[pl.BlockSpec((SUBLANES, d), lambda l, j: (0, 0))] + w_specs
        + [pl.BlockSpec((None, 1, ADA_TN), lambda l, j: (l, 0, j))],
        out_specs=pl.BlockSpec((None, SUBLANES, ADA_TN), lambda l, j: (l, 0, j)),
        compiler_params=_params(("arbitrary", "arbitrary")),
        name="ada_mod",
    )(c_pad, *([w_ada] * ADA_K_SPLIT), b_ada.reshape(depth, 1, n))


def _modulate_kernel(x_ref, sc_ref, sh_ref, u_ref):
    u_ref[...] = (x_ref[...] * (1.0 + sc_ref[...]) + sh_ref[...]).astype(BF16)


def _modulate(x2, sc, sh, seq):
    m, d = x2.shape
    per_b = seq // ROW_TM
    vec = pl.BlockSpec((None, 1, d), lambda i: (i // per_b, 0, 0))
    return pl.pallas_call(
        _modulate_kernel,
        out_shape=jax.ShapeDtypeStruct((m, d), BF16),
        grid=(m // ROW_TM,),
        in_specs=[pl.BlockSpec((ROW_TM, d), lambda i: (i, 0)), vec, vec],
        out_specs=pl.BlockSpec((ROW_TM, d), lambda i: (i, 0)),
        compiler_params=_params(("arbitrary",)),
        name="modulate",
    )(x2, sc, sh)


def _rms_rope(xh, gain, cos, sin_a, sin_b):
    r = lax.rsqrt(jnp.mean(xh * xh, axis=-1, keepdims=True) + RMS_EPS)
    xn = xh * r * gain
    return xn * cos + pltpu.roll(xn, 96, 1) * sin_a + pltpu.roll(xn, 32, 1) * sin_b


_Q_TILES = D_ATTN // INPROJ_TN
_K_TILE = _Q_TILES
_V_TILE = _K_TILE + 1
_R_TILE0 = _V_TILE + 1
_R_TILES = D_RNN // INPROJ_TN
_G_TILE0 = _R_TILE0 + _R_TILES
assert D_KV == INPROJ_TN and D_IN == (_G_TILE0 + _R_TILES) * INPROJ_TN


def _inproj_kernel(u_ref, w_ref, g_ref, cos_ref, sa_ref, sb_ref,
                   q_ref, k_ref, vt_ref, xr_ref, xg_ref):
    j = pl.program_id(1)
    z = jnp.dot(u_ref[...], w_ref[...], preferred_element_type=F32)
    heads = INPROJ_TN // HEAD_DIM

    def head(h):
        return z[:, h * HEAD_DIM:(h + 1) * HEAD_DIM]

    @pl.when(j < _Q_TILES)
    def _():
        cos, sa, sb = cos_ref[...], sa_ref[...], sb_ref[...]
        for h in range(heads):
            q_ref[h] = (_rms_rope(head(h), g_ref[0:1, :], cos, sa, sb) * Q_SCALE).astype(BF16)

    @pl.when(j == _K_TILE)
    def _():
        cos, sa, sb = cos_ref[...], sa_ref[...], sb_ref[...]
        for h in range(heads):
            k_ref[h] = _rms_rope(head(h), g_ref[1:2, :], cos, sa, sb).astype(BF16)

    @pl.when(j == _V_TILE)
    def _():
        for h in range(heads):
            vt_ref[h, 0:HEAD_DIM, :] = head(h).T.astype(BF16)
            vt_ref[h, HEAD_DIM:, :] = jnp.ones((VT_ROWS - HEAD_DIM, z.shape[0]), BF16)

    @pl.when((j >= _R_TILE0) & (j < _G_TILE0))
    def _():
        xr_ref[...] = z.astype(BF16)

    @pl.when(j >= _G_TILE0)
    def _():
        xg_ref[...] = z.astype(BF16)


def _inproj(u, w_in, layer, qk_g, cos, sin_a, sin_b, seq):
    m, d = u.shape
    tm, tn = INPROJ_TM, INPROJ_TN
    per_b = seq // tm
    heads = tn // HEAD_DIM
    tab = pl.BlockSpec((tm, HEAD_DIM), lambda i, j: (i % per_b, 0))
    return pl.pallas_call(
        _inproj_kernel,
        out_shape=(
            jax.ShapeDtypeStruct((N_Q_HEADS, m, HEAD_DIM), BF16),
            jax.ShapeDtypeStruct((N_KV_HEADS, m, HEAD_DIM), BF16),
            jax.ShapeDtypeStruct((N_KV_HEADS, VT_ROWS, m), BF16),
            jax.ShapeDtypeStruct((m, D_RNN), BF16),
            jax.ShapeDtypeStruct((m, D_RNN), BF16),
        ),
        grid=(m // tm, D_IN // tn),
        in_specs=[
            pl.BlockSpec((tm, d), lambda i, j: (i, 0)),
            pl.BlockSpec((None, d, tn), lambda i, j: (layer, 0, j)),
            pl.BlockSpec((2, HEAD_DIM), lambda i, j: (0, 0)),
            tab, tab, tab,
        ],
        out_specs=(
            pl.BlockSpec((heads, tm, HEAD_DIM), lambda i, j: (jnp.minimum(j, _Q_TILES - 1), i, 0)),
            pl.BlockSpec((N_KV_HEADS, tm, HEAD_DIM), lambda i, j: (0, i, 0)),
            pl.BlockSpec((N_KV_HEADS, VT_ROWS, tm), lambda i, j: (0, 0, i)),
            pl.BlockSpec((tm, tn), lambda i, j: (i, jnp.clip(j - _R_TILE0, 0, _R_TILES - 1))),
            pl.BlockSpec((tm, tn), lambda i, j: (i, jnp.clip(j - _G_TILE0, 0, _R_TILES - 1))),
        ),
        compiler_params=_params(("arbitrary", "arbitrary")),
        name="in_proj",
    )(u, w_in, qk_g, cos, sin_a, sin_b)


def _attn_kernel(q_ref, k_ref, vt_ref, g_ref, o_ref, m_s, acc_s):
    j = pl.program_id(2)
    tq = q_ref.shape[1]
    tk = k_ref.shape[1]
    rows = tk // ATTN_SCORE_SPLIT

    @pl.when(j == 0)
    def _():
        m_s[...] = jnp.full(m_s.shape, -jnp.inf, F32)
        acc_s[...] = jnp.zeros(acc_s.shape, F32)

    def scores(kh, g):
        q = q_ref[Q_PER_KV * kh + g]
        return [lax.dot_general(k_ref[kh, r * rows:(r + 1) * rows, :], q, (((1,), (1,)), ((), ())),
                                preferred_element_type=F32) for r in range(ATTN_SCORE_SPLIT)]

    def update(kh, g, s_parts):
        cols = slice(g * tq, (g + 1) * tq)
        m_old = m_s[kh:kh + 1, cols]
        m_new = m_old
        for s_t in s_parts:
            m_new = jnp.maximum(m_new, jnp.max(s_t, axis=0, keepdims=True))
        alpha = jnp.exp2(m_old - m_new)
        p = jnp.concatenate([jnp.exp2(s_t - m_new).astype(BF16) for s_t in s_parts], axis=0)
        pv = jnp.dot(vt_ref[kh], p, preferred_element_type=F32)
        acc_s[kh, :, cols] = alpha * acc_s[kh, :, cols] + pv
        m_s[kh:kh + 1, cols] = m_new

    blocks = [(kh, g) for kh in range(N_KV_HEADS) for g in range(Q_PER_KV)]
    pending = [scores(*blk) for blk in blocks[:ATTN_LOOKAHEAD]]
    for c, (kh, g) in enumerate(blocks):
        if c + ATTN_LOOKAHEAD < len(blocks):
            pending.append(scores(*blocks[c + ATTN_LOOKAHEAD]))
        update(kh, g, pending.pop(0))

    @pl.when(j == pl.num_programs(2) - 1)
    def _():
        outs = []
        ss = jnp.zeros((1, tq), F32)
        for kh in range(N_KV_HEADS):
            o = acc_s[kh, 0:HEAD_DIM, :] / acc_s[kh, HEAD_DIM:HEAD_DIM + 1, :]
            sq = jnp.sum(o * o, axis=0, keepdims=True)
            for g in range(Q_PER_KV):
                ss = ss + sq[:, g * tq:(g + 1) * tq]
            outs.append(o)
        r = lax.rsqrt(ss * (1.0 / D_ATTN) + RMS_EPS)
        for kh in range(N_KV_HEADS):
            for g in range(Q_PER_KV):
                h = kh * Q_PER_KV + g
                oh = (outs[kh][:, g * tq:(g + 1) * tq] * r).T
                o_ref[:, h * HEAD_DIM:(h + 1) * HEAD_DIM] = (
                    oh * g_ref[:, h * HEAD_DIM:(h + 1) * HEAD_DIM]).astype(BF16)


def _attention(q, k, vt, out_g_attn, batch, seq):
    tq, tk = ATTN_TQ, ATTN_TK
    nq, nk = seq // tq, seq // tk
    m = batch * seq
    return pl.pallas_call(
        _attn_kernel,
        out_shape=jax.ShapeDtypeStruct((m, D_ATTN), BF16),
        grid=(batch, nq, nk),
        in_specs=[
            pl.BlockSpec((N_Q_HEADS, tq, HEAD_DIM), lambda b, i, j: (0, b * nq + i, 0)),
            pl.BlockSpec((N_KV_HEADS, tk, HEAD_DIM), lambda b, i, j: (0, b * nk + j, 0)),
            pl.BlockSpec((N_KV_HEADS, VT_ROWS, tk), lambda b, i, j: (0, 0, b * nk + j)),
            pl.BlockSpec((1, D_ATTN), lambda b, i, j: (0, 0)),
        ],
        out_specs=pl.BlockSpec((tq, D_ATTN), lambda b, i, j: (b * nq + i, 0)),
        scratch_shapes=[
            pltpu.VMEM((SUBLANES, Q_PER_KV * tq), F32),
            pltpu.VMEM((N_KV_HEADS, VT_ROWS, Q_PER_KV * tq), F32),
        ],
        compiler_params=_params(("arbitrary", "arbitrary", "arbitrary")),
        name="attention",
    )(q, k, vt, out_g_attn)


def _gelu_tanh(x):
    return 0.5 * x * (1.0 + jnp.tanh(math.sqrt(2.0 / math.pi) * (x + 0.044715 * (x * x * x))))


def _softplus(x):
    return jnp.maximum(x, 0.0) + jnp.log1p(jnp.exp(-jnp.abs(x)))


def _rglru_kernel(xr_ref, xg_ref, cw_ref, cb_ref, wa_ref, ba_ref, wi_ref, bi_ref, lam_ref,
                  y_ref, xe_s, a_s, u_s):
    seq = xr_ref.shape[0]
    ns = RNN_SEGMENTS
    steps = seq // ns
    t = RNN_CHUNK
    n_chunks = seq // t
    chains = ns // SUBLANES

    cw = cw_ref[...]
    cb = cb_ref[...]
    coef = -RG_C * _softplus(-lam_ref[...])
    w_a = [wa_ref[d].astype(BF16) for d in range(2)]
    w_i = [wi_ref[d].astype(BF16) for d in range(2)]

    seg_id = lax.broadcasted_iota(jnp.int32, (ns, LANES), 0)
    xe_s[pl.ds(0, ns), :] = jnp.where(
        seg_id == 0, 0.0, pltpu.roll(xr_ref[pl.ds(seq - ns, ns), :].astype(F32), 1, 0))
    for k in range(2):
        xe_s[pl.ds(ns + seq + k * ns, ns), :] = jnp.where(
            seg_id == ns - 1, 0.0, pltpu.roll(xr_ref[pl.ds(k * ns, ns), :].astype(F32), ns - 1, 0))

    def stage(c, carry):
        t0 = pl.multiple_of(c * t, t)
        xe_s[pl.ds(ns + t0, t), :] = xr_ref[pl.ds(t0, t), :].astype(F32)
        return carry

    lax.fori_loop(0, n_chunks, stage, 0)

    def gates(c, carry):
        t0 = pl.multiple_of(c * t, t)
        xc = cb
        for tap in range(CONV_W):
            xc = xc + cw[tap:tap + 1, :] * xe_s[pl.ds(t0 + tap * ns, t), :]
        xcb = xc.astype(BF16)
        for d in range(2):
            r = jax.nn.sigmoid(jnp.dot(xcb, w_a[d], preferred_element_type=F32) + ba_ref[d:d + 1, :])
            i = jax.nn.sigmoid(jnp.dot(xcb, w_i[d], preferred_element_type=F32) + bi_ref[d:d + 1, :])
            log_a = coef[d:d + 1, :] * r
            a = jnp.exp(log_a)
            a_s[d, pl.ds(t0, t), :] = a
            u_s[d, pl.ds(t0, t), :] = jnp.sqrt(-jnp.tanh(log_a) * (a * a + 1.0)) * i * xc
        return carry

    lax.fori_loop(0, n_chunks, gates, 0)

    def seg_rows(g, j):
        return pl.ds(pl.multiple_of(j * ns + g * SUBLANES, SUBLANES), SUBLANES)

    def local(j, carry):
        out = []
        for d in range(2):
            jj = j if d == 0 else steps - 1 - j
            for g in range(chains):
                h, p = carry[d * chains + g]
                a = a_s[d, seg_rows(g, jj), :]
                out.append((a * h + u_s[d, seg_rows(g, jj), :], a * p))
        return tuple(out)

    zero = jnp.zeros((SUBLANES, LANES), F32)
    one = jnp.ones((SUBLANES, LANES), F32)
    ends = lax.fori_loop(0, steps, local, tuple((zero, one) for _ in range(2 * chains)), unroll=8)

    starts = []
    for d in range(2):
        h_end = jnp.concatenate([ends[d * chains + g][0] for g in range(chains)], axis=0)
        p_end = jnp.concatenate([ends[d * chains + g][1] for g in range(chains)], axis=0)
        order = range(ns) if d == 0 else range(ns - 1, -1, -1)
        carry = jnp.zeros((1, LANES), F32)
        rows = [None] * ns
        for s in order:
            rows[s] = carry
            carry = p_end[s:s + 1, :] * carry + h_end[s:s + 1, :]
        init = jnp.concatenate(rows, axis=0)
        starts.extend(init[g * SUBLANES:(g + 1) * SUBLANES] for g in range(chains))

    def final(j, carry):
        out = []
        for d in range(2):
            jj = j if d == 0 else steps - 1 - j
            for g in range(chains):
                h = a_s[d, seg_rows(g, jj), :] * carry[d * chains + g] + u_s[d, seg_rows(g, jj), :]
                u_s[d, seg_rows(g, jj), :] = h
                out.append(h)
        return tuple(out)

    lax.fori_loop(0, steps, final, tuple(starts), unroll=8)

    def combine(c, carry):
        t0 = pl.multiple_of(c * t, t)
        h = u_s[0, pl.ds(t0, t), :] + u_s[1, pl.ds(t0, t), :]
        y_ref[pl.ds(t0, t), :] = _gelu_tanh(xg_ref[pl.ds(t0, t), :].astype(F32)) * h
        return carry

    lax.fori_loop(0, n_chunks, combine, 0)


def _rglru(xr, xg, conv_w, conv_b, w_a, b_a, w_i, b_i, lam, batch, seq):
    m, d_rnn = xr.shape
    blk = RNN_BLOCK
    assert seq % RNN_CHUNK == 0 and RNN_CHUNK % RNN_SEGMENTS == 0 and RNN_SEGMENTS % SUBLANES == 0
    slab = pl.BlockSpec((seq, blk), lambda b, n: (b, n))
    vec2 = pl.BlockSpec((2, blk), lambda b, n: (0, n))
    wspec = pl.BlockSpec((2, None, blk, blk), lambda b, n: (0, n, 0, 0))
    return pl.pallas_call(
        _rglru_kernel,
        out_shape=jax.ShapeDtypeStruct((m, d_rnn), F32),
        grid=(batch, d_rnn // blk),
        in_specs=[
            slab, slab,
            pl.BlockSpec((CONV_W, blk), lambda b, n: (0, n)),
            pl.BlockSpec((1, blk), lambda b, n: (0, n)),
            wspec, vec2, wspec, vec2, vec2,
        ],
        out_specs=slab,
        scratch_shapes=[pltpu.VMEM((seq + (CONV_W - 1) * RNN_SEGMENTS, blk), F32),
                        pltpu.VMEM((2, seq, blk), F32), pltpu.VMEM((2, seq, blk), F32)],
        compiler_params=_params(("arbitrary", "arbitrary")),
        name="rg_lru",
    )(xr, xg, conv_w, conv_b.reshape(1, d_rnn), w_a, b_a, w_i, b_i, lam)


def _rms_rows_kernel(y_ref, g_ref, o_ref):
    y = y_ref[...]
    r = lax.rsqrt(jnp.mean(y * y, axis=-1, keepdims=True) + RMS_EPS)
    o_ref[...] = (y * r * g_ref[...]).astype(BF16)


def _rms_rows(y, gain):
    m, d = y.shape
    return pl.pallas_call(
        _rms_rows_kernel,
        out_shape=jax.ShapeDtypeStruct((m, d), BF16),
        grid=(m // ROW_TM,),
        in_specs=[pl.BlockSpec((ROW_TM, d), lambda i: (i, 0)), pl.BlockSpec((1, d), lambda i: (0, 0))],
        out_specs=pl.BlockSpec((ROW_TM, d), lambda i: (i, 0)),
        compiler_params=_params(("arbitrary",)),
        name="rnn_rms",
    )(y, gain)


def _outproj_kernel(a1_ref, a2_ref, w_ref, o_ref):
    k1 = a1_ref.shape[1]
    o_ref[...] = (jnp.dot(a1_ref[...], w_ref[0:k1, :], preferred_element_type=F32)
                  + jnp.dot(a2_ref[...], w_ref[k1:, :], preferred_element_type=F32)).astype(o_ref.dtype)


def _outproj(a1, a2, w_o, layer):
    m, k1 = a1.shape
    k2 = a2.shape[1]
    n = w_o.shape[2]
    return pl.pallas_call(
        _outproj_kernel,
        out_shape=jax.ShapeDtypeStruct((m, n), BF16),
        grid=(m // MM_TM, n // MM_TN),
        in_specs=[
            pl.BlockSpec((MM_TM, k1), lambda i, j: (i, 0)),
            pl.BlockSpec((MM_TM, k2), lambda i, j: (i, 0)),
            pl.BlockSpec((None, k1 + k2, MM_TN), lambda i, j: (layer, 0, j)),
        ],
        out_specs=pl.BlockSpec((MM_TM, MM_TN), lambda i, j: (i, j)),
        compiler_params=_params(("arbitrary", "arbitrary")),
        name="out_proj",
    )(a1, a2, w_o)


def _deepnorm_kernel(x_ref, y_ref, gate_ref, g_ref, b_ref, *rest, alpha, modulate):
    s = alpha * x_ref[...] + gate_ref[...] * y_ref[...].astype(F32)
    mu = jnp.mean(s, axis=-1, keepdims=True)
    sc = s - mu
    var = jnp.mean(sc * sc, axis=-1, keepdims=True)
    out = sc * lax.rsqrt(var + LN_EPS) * g_ref[...] + b_ref[...]
    if modulate:
        sc_ref, sh_ref, o_ref, u_ref = rest
        u_ref[...] = (out * (1.0 + sc_ref[...]) + sh_ref[...]).astype(BF16)
        o_ref[...] = out
    else:
        (o_ref,) = rest
        ns = o_ref.shape[0]
        for jj in range(SUBLANES):
            o_ref[:, jj, :] = out[jj * ns:(jj + 1) * ns, :]


def _deepnorm(x2, y, gate, ln_g, ln_b, nxt, alpha, seq):
    m, d = x2.shape
    per_b = seq // ROW_TM
    rows = pl.BlockSpec((ROW_TM, d), lambda i: (i, 0))
    vec_b = pl.BlockSpec((None, 1, d), lambda i: (i // per_b, 0, 0))
    vec = pl.BlockSpec((1, d), lambda i: (0, 0))
    modulate = nxt is not None
    in_specs = [rows, rows, vec_b, vec, vec]
    args = [x2, y, gate, ln_g.reshape(1, d), ln_b.reshape(1, d)]
    if modulate:
        in_specs += [vec_b, vec_b]
        args += list(nxt)
        out_shape = [jax.ShapeDtypeStruct((m, d), F32), jax.ShapeDtypeStruct((m, d), BF16)]
        out_specs = [rows, rows]
    else:
        ns = RNN_SEGMENTS
        assert ROW_TM == SUBLANES * ns
        out_shape = [jax.ShapeDtypeStruct((m // seq, ns, seq // ns, d), F32)]
        out_specs = [pl.BlockSpec((None, ns, SUBLANES, d), lambda i: (i // per_b, 0, i % per_b, 0))]
    res = pl.pallas_call(
        functools.partial(_deepnorm_kernel, alpha=alpha, modulate=modulate),
        out_shape=tuple(out_shape),
        grid=(m // ROW_TM,),
        in_specs=in_specs,
        out_specs=tuple(out_specs),
        compiler_params=_params(("arbitrary",)),
        name="deepnorm",
    )(*args)
    return res if modulate else (res[0], None)


def _ffn_in_kernel(u_ref, wg_ref, *refs):
    up_refs, h_ref = refs[:-1], refs[-1]
    u = u_ref[...]
    gate = jnp.dot(u, wg_ref[...], preferred_element_type=F32)
    up = jnp.concatenate([jnp.dot(u, w_ref[...], preferred_element_type=F32) for w_ref in up_refs],
                         axis=1)
    h_ref[...] = (gate * jax.nn.sigmoid(gate) * up).astype(BF16)


def _ffn_in(u, w_ffn, layer, d_ff):
    m, d = u.shape
    assert w_ffn.shape[2] == 2 * d_ff and d_ff % FFN_UP_TN == 0 and FFN_TN % FFN_UP_TN == 0
    parts = FFN_TN // FFN_UP_TN
    up0 = d_ff // FFN_UP_TN
    last = 2 * up0 - 1
    up_specs = [pl.BlockSpec((None, d, FFN_UP_TN), functools.partial(
        lambda i, j, p: (layer, 0, jnp.minimum(up0 + parts * j + p, last)), p=p))
                for p in range(parts)]
    return pl.pallas_call(
        _ffn_in_kernel,
        out_shape=jax.ShapeDtypeStruct((m, d_ff), BF16),
        grid=(m // MM_TM, pl.cdiv(d_ff, FFN_TN)),
        in_specs=[pl.BlockSpec((MM_TM, d), lambda i, j: (i, 0)),
                  pl.BlockSpec((None, d, FFN_TN), lambda i, j: (layer, 0, j))] + up_specs,
        out_specs=pl.BlockSpec((MM_TM, FFN_TN), lambda i, j: (i, j)),
        compiler_params=_params(("arbitrary", "arbitrary")),
        name="ffn_in",
    )(u, w_ffn, *([w_ffn] * parts))


def _ffn_down_kernel(h_ref, w_ref, o_ref):
    o_ref[...] = jnp.dot(h_ref[...], w_ref[...], preferred_element_type=F32).astype(o_ref.dtype)


def _ffn_down(h, w_down, layer):
    m, kdim = h.shape
    n = w_down.shape[2]
    return pl.pallas_call(
        _ffn_down_kernel,
        out_shape=jax.ShapeDtypeStruct((m, n), BF16),
        grid=(n // DOWN_TN, m // DOWN_TM),
        in_specs=[
            pl.BlockSpec((DOWN_TM, kdim), lambda j, i: (i, 0)),
            pl.BlockSpec((None, kdim, DOWN_TN), lambda j, i: (layer, 0, j),
                         pipeline_mode=pl.Buffered(1)),
        ],
        out_specs=pl.BlockSpec((DOWN_TM, DOWN_TN), lambda j, i: (i, j)),
        compiler_params=_params(("arbitrary", "arbitrary")),
        name="ffn_down",
    )(h, w_down)


def _rope_tables(seq):
    pos = jnp.arange(seq, dtype=jnp.int32)
    row = (pos // GRID_W).astype(F32)
    col = (pos % GRID_W).astype(F32)
    inv_freq = jnp.power(ROPE_THETA, -jnp.arange(0, ROPE_AXIS_DIM, 2, dtype=F32) / ROPE_AXIS_DIM)
    ang_r = row[:, None] * inv_freq[None, :]
    ang_c = col[:, None] * inv_freq[None, :]
    cos_r, sin_r, cos_c, sin_c = jnp.cos(ang_r), jnp.sin(ang_r), jnp.cos(ang_c), jnp.sin(ang_c)
    zero = jnp.zeros_like(sin_r)
    cos = jnp.concatenate([cos_r, cos_r, cos_c, cos_c], axis=-1)
    sin_a = jnp.concatenate([-sin_r, zero, -sin_c, zero], axis=-1)
    sin_b = jnp.concatenate([zero, sin_r, zero, sin_c], axis=-1)
    return cos, sin_a, sin_b


def kernel(x, c, w_ada, b_ada, w_in, qk_norm_g, conv_w, conv_b, rg_w_a, rg_b_a, rg_w_i, rg_b_i,
           rg_lam, mix_out_g, w_o, ln_g, ln_b, w_ffn_in, w_down):
    batch, seq, d = x.shape
    depth = w_ada.shape[0]
    m = batch * seq
    alpha = (2 * depth) ** 0.25
    assert d == D_MODEL and seq % ATTN_TK == 0 and seq % INPROJ_TM == 0 and m % MM_TM == 0
    assert m % DOWN_TM == 0 and d % DOWN_TN == 0

    ns = RNN_SEGMENTS

    def interleave(a):
        lead = a.shape[:-2]
        return a.reshape(*lead, ns, seq // ns, a.shape[-1]).swapaxes(-3, -2).reshape(a.shape)

    cos, sin_a, sin_b = (interleave(tab) for tab in _rope_tables(seq))
    c_pad = jnp.pad(c, ((0, SUBLANES - batch), (0, 0)))
    mod = _ada(c_pad, w_ada, b_ada)[:, :batch].reshape(depth, batch, 6, 1, d)

    w_in_b = w_in.astype(BF16)
    w_o_b = w_o.astype(BF16)
    w_ffn_b = w_ffn_in.astype(BF16)
    w_down_b = w_down.astype(BF16)

    x2 = interleave(x).reshape(m, d)
    u = _modulate(x2, mod[0, :, 1], mod[0, :, 0], seq)
    for l in range(depth):
        sh2, sc2, g1, g2 = mod[l, :, 3], mod[l, :, 4], mod[l, :, 2], mod[l, :, 5]
        q, k, vt, xr, xg = _inproj(u, w_in_b, l, qk_norm_g[l], cos, sin_a, sin_b, seq)
        y_attn = _attention(q, k, vt, mix_out_g[l, :D_ATTN].reshape(1, D_ATTN), batch, seq)
        y_rnn = _rglru(xr, xg, conv_w[l], conv_b[l], rg_w_a[l], rg_b_a[l], rg_w_i[l], rg_b_i[l],
                       rg_lam[l], batch, seq)
        y_rnn = _rms_rows(y_rnn, mix_out_g[l, D_ATTN:].reshape(1, D_RNN))
        y = _outproj(y_attn, y_rnn, w_o_b, l)
        x2, u = _deepnorm(x2, y, g1, ln_g[l, 0], ln_b[l, 0], (sc2, sh2), alpha, seq)
        h = _ffn_in(u, w_ffn_b, l, D_FF)
        y = _ffn_down(h, w_down_b, l)
        nxt = (mod[l + 1, :, 1], mod[l + 1, :, 0]) if l + 1 < depth else None
        x2, u = _deepnorm(x2, y, g2, ln_g[l, 1], ln_b[l, 1], nxt, alpha, seq)
    return x2.reshape(batch, seq, d)
```

```python
import functools
import math

import jax
import jax.numpy as jnp
from jax import lax
from jax.experimental import pallas as pl
from jax.experimental.pallas import tpu as pltpu

D_MODEL = 4096
GRID_W = 64
HEAD_DIM = 128
D_ATTN = D_MODEL // 2
N_Q_HEADS = D_ATTN // HEAD_DIM
N_KV_HEADS = 4
Q_PER_KV = N_Q_HEADS // N_KV_HEADS
D_KV = N_KV_HEADS * HEAD_DIM
D_RNN = D_MODEL - D_ATTN
RNN_BLOCK = 128
N_RNN_BLOCKS = D_RNN // RNN_BLOCK
D_IN = D_ATTN + 2 * D_KV + 2 * D_RNN
CONV_W = 4
RG_C = 8.0
ROPE_THETA = 10000.0
ROPE_AXIS_DIM = HEAD_DIM // 2
D_FF = (-(-8 * D_MODEL // (3 * 256))) * 256
LN_EPS = 1e-5
RMS_EPS = 1e-6
Q_SCALE = HEAD_DIM ** -0.5 * math.log2(math.e)

V7X_VMEM_BYTES = 64 * 1024 * 1024
VMEM_LIMIT = V7X_VMEM_BYTES - 8 * 1024 * 1024
LANES = 128
SUBLANES = 8

BF16 = jnp.bfloat16
F32 = jnp.float32

ADA_TN = 1024
ADA_K_SPLIT = 4
INPROJ_TM = 1024
INPROJ_TN = 512
INPROJ_ROW_CHUNKS = 4
ATTN_TQ = 512
ATTN_TK = 1024
ATTN_SCORE_SPLIT = 4
ATTN_LOOKAHEAD = 2
VT_ROWS = HEAD_DIM + 16
MM_TM = 1024
MM_TN = 1024
FFN_TN = 512
FFN_UP_TN = 256
DOWN_TM = 512
DOWN_TN = 1024
RNN_SEGMENTS = 32
ROW_TM = 256
RNN_CHUNK = 512


def _params(sem):
    return pltpu.CompilerParams(dimension_semantics=sem, vmem_limit_bytes=VMEM_LIMIT)


def _ada_kernel(c_ref, *refs):
    w_refs, (b_ref, o_ref) = refs[:ADA_K_SPLIT], refs[ADA_K_SPLIT:]
    c = c_ref[...]
    c_act = (c * jax.nn.sigmoid(c)).astype(BF16)
    kc = w_refs[0].shape[0]
    acc = b_ref[...]
    for s, w_ref in enumerate(w_refs):
        acc = acc + jnp.dot(c_act[:, s * kc:(s + 1) * kc], w_ref[...].astype(BF16),
                            preferred_element_type=F32)
    o_ref[...] = acc


def _ada(c_pad, w_ada, b_ada):
    depth, d, n = w_ada.shape
    kc = d // ADA_K_SPLIT
    w_specs = [pl.BlockSpec((None, kc, ADA_TN), functools.partial(lambda l, j, s: (l, s, j), s=s))
               for s in range(ADA_K_SPLIT)]
    return pl.pallas_call(
        _ada_kernel,
        out_shape=jax.ShapeDtypeStruct((depth, SUBLANES, n), F32),
        grid=(depth, n // ADA_TN),
        in_specs=[pl.BlockSpec((SUBLANES, d), lambda l, j: (0, 0))] + w_specs
        + [pl.BlockSpec((None, 1, ADA_TN), lambda l, j: (l, 0, j))],
        out_specs=pl.BlockSpec((None, SUBLANES, ADA_TN), lambda l, j: (l, 0, j)),
        compiler_params=_params(("arbitrary", "arbitrary")),
        name="ada_mod",
    )(c_pad, *([w_ada] * ADA_K_SPLIT), b_ada.reshape(depth, 1, n))


def _modulate_kernel(x_ref, sc_ref, sh_ref, u_ref):
    u_ref[...] = (x_ref[...] * (1.0 + sc_ref[...]) + sh_ref[...]).astype(BF16)


def _modulate(x2, sc, sh, seq):
    m, d = x2.shape
    per_b = seq // ROW_TM
    vec = pl.BlockSpec((None, 1, d), lambda i: (i // per_b, 0, 0))
    return pl.pallas_call(
        _modulate_kernel,
        out_shape=jax.ShapeDtypeStruct((m, d), BF16),
        grid=(m // ROW_TM,),
        in_specs=[pl.BlockSpec((ROW_TM, d), lambda i: (i, 0)), vec, vec],
        out_specs=pl.BlockSpec((ROW_TM, d), lambda i: (i, 0)),
        compiler_params=_params(("arbitrary",)),
        name="modulate",
    )(x2, sc, sh)


def _rms_rope(xh, gain, cos, sin_a, sin_b):
    r = lax.rsqrt(jnp.mean(xh * xh, axis=-1, keepdims=True) + RMS_EPS)
    xn = xh * r * gain
    return xn * cos + pltpu.roll(xn, 96, 1) * sin_a + pltpu.roll(xn, 32, 1) * sin_b


_Q_TILES = D_ATTN // INPROJ_TN
_K_TILE = _Q_TILES
_V_TILE = _K_TILE + 1
_R_TILE0 = _V_TILE + 1
_R_TILES = D_RNN // INPROJ_TN
_G_TILE0 = _R_TILE0 + _R_TILES
assert D_KV == INPROJ_TN and D_IN == (_G_TILE0 + _R_TILES) * INPROJ_TN


def _inproj_kernel(u_ref, w_ref, g_ref, cos_ref, sa_ref, sb_ref,
                   q_ref, k_ref, vt_ref, xr_ref, xg_ref):
    j = pl.program_id(1)
    heads = INPROJ_TN // HEAD_DIM
    rows = u_ref.shape[0] // INPROJ_ROW_CHUNKS

    def row_chunks(epilogue):
        def z_chunk(c):
            return jnp.dot(u_ref[c * rows:(c + 1) * rows, :], w_ref[...], preferred_element_type=F32)

        z_next = z_chunk(0)
        for c in range(INPROJ_ROW_CHUNKS):
            z = z_next
            if c + 1 < INPROJ_ROW_CHUNKS:
                z_next = z_chunk(c + 1)
            epilogue(slice(c * rows, (c + 1) * rows), z)

    def head(z, h):
        return z[:, h * HEAD_DIM:(h + 1) * HEAD_DIM]

    def rope_heads(out_ref, gain_row, scale):
        def epilogue(r, z):
            cos, sa, sb = cos_ref[r, :], sa_ref[r, :], sb_ref[r, :]
            gain = g_ref[gain_row:gain_row + 1, :]
            for h in range(heads):
                x = _rms_rope(head(z, h), gain, cos, sa, sb)
                out_ref[h, r, :] = (x if scale is None else x * scale).astype(BF16)
        return epilogue

    @pl.when(j < _Q_TILES)
    def _():
        row_chunks(rope_heads(q_ref, 0, Q_SCALE))

    @pl.when(j == _K_TILE)
    def _():
        row_chunks(rope_heads(k_ref, 1, None))

    @pl.when(j == _V_TILE)
    def _():
        def epilogue(r, z):
            for h in range(heads):
                vt_ref[h, 0:HEAD_DIM, r] = head(z, h).T.astype(BF16)
                vt_ref[h, HEAD_DIM:, r] = jnp.ones((VT_ROWS - HEAD_DIM, rows), BF16)
        row_chunks(epilogue)

    def plain(out_ref):
        def epilogue(r, z):
            out_ref[r, :] = z.astype(BF16)
        return epilogue

    @pl.when((j >= _R_TILE0) & (j < _G_TILE0))
    def _():
        row_chunks(plain(xr_ref))

    @pl.when(j >= _G_TILE0)
    def _():
        row_chunks(plain(xg_ref))


def _inproj(u, w_in, layer, qk_g, cos, sin_a, sin_b, seq):
    m, d = u.shape
    tm, tn = INPROJ_TM, INPROJ_TN
    per_b = seq // tm
    heads = tn // HEAD_DIM
    tab = pl.BlockSpec((tm, HEAD_DIM), lambda i, j: (i % per_b, 0))
    return pl.pallas_call(
        _inproj_kernel,
        out_shape=(
            jax.ShapeDtypeStruct((N_Q_HEADS, m, HEAD_DIM), BF16),
            jax.ShapeDtypeStruct((N_KV_HEADS, m, HEAD_DIM), BF16),
            jax.ShapeDtypeStruct((N_KV_HEADS, VT_ROWS, m), BF16),
            jax.ShapeDtypeStruct((m, D_RNN), BF16),
            jax.ShapeDtypeStruct((m, D_RNN), BF16),
        ),
        grid=(m // tm, D_IN // tn),
        in_specs=[
            pl.BlockSpec((tm, d), lambda i, j: (i, 0)),
            pl.BlockSpec((None, d, tn), lambda i, j: (layer, 0, j)),
            pl.BlockSpec((2, HEAD_DIM), lambda i, j: (0, 0)),
            tab, tab, tab,
        ],
        out_specs=(
            pl.BlockSpec((heads, tm, HEAD_DIM), lambda i, j: (jnp.minimum(j, _Q_TILES - 1), i, 0)),
            pl.BlockSpec((N_KV_HEADS, tm, HEAD_DIM), lambda i, j: (0, i, 0)),
            pl.BlockSpec((N_KV_HEADS, VT_ROWS, tm), lambda i, j: (0, 0, i)),
            pl.BlockSpec((tm, tn), lambda i, j: (i, jnp.clip(j - _R_TILE0, 0, _R_TILES - 1))),
            pl.BlockSpec((tm, tn), lambda i, j: (i, jnp.clip(j - _G_TILE0, 0, _R_TILES - 1))),
        ),
        compiler_params=_params(("arbitrary", "arbitrary")),
        name="in_proj",
    )(u, w_in, qk_g, cos, sin_a, sin_b)


def _attn_kernel(q_ref, k_ref, vt_ref, g_ref, o_ref, m_s, acc_s):
    j = pl.program_id(2)
    tq = q_ref.shape[1]
    tk = k_ref.shape[1]
    rows = tk // ATTN_SCORE_SPLIT

    @pl.when(j == 0)
    def _():
        m_s[...] = jnp.full(m_s.shape, -jnp.inf, F32)
        acc_s[...] = jnp.zeros(acc_s.shape, F32)

    def scores(kh, g):
        q = q_ref[Q_PER_KV * kh + g]
        return [lax.dot_general(k_ref[kh, r * rows:(r + 1) * rows, :], q, (((1,), (1,)), ((), ())),
                                preferred_element_type=F32) for r in range(ATTN_SCORE_SPLIT)]

    def update(kh, g, s_parts):
        cols = slice(g * tq, (g + 1) * tq)
        m_old = m_s[kh:kh + 1, cols]
        m_new = m_old
        for s_t in s_parts:
            m_new = jnp.maximum(m_new, jnp.max(s_t, axis=0, keepdims=True))
        alpha = jnp.exp2(m_old - m_new)
        p = jnp.concatenate([jnp.exp2(s_t - m_new).astype(BF16) for s_t in s_parts], axis=0)
        pv = jnp.dot(vt_ref[kh], p, preferred_element_type=F32)
        acc_s[kh, :, cols] = alpha * acc_s[kh, :, cols] + pv
        m_s[kh:kh + 1, cols] = m_new

    blocks = [(kh, g) for kh in range(N_KV_HEADS) for g in range(Q_PER_KV)]
    pending = [scores(*blk) for blk in blocks[:ATTN_LOOKAHEAD]]
    for c, (kh, g) in enumerate(blocks):
        if c + ATTN_LOOKAHEAD < len(blocks):
            pending.append(scores(*blocks[c + ATTN_LOOKAHEAD]))
        update(kh, g, pending.pop(0))

    @pl.when(j == pl.num_programs(2) - 1)
    def _():
        outs = []
        ss = jnp.zeros((1, tq), F32)
        for kh in range(N_KV_HEADS):
            o = acc_s[kh, 0:HEAD_DIM, :] / acc_s[kh, HEAD_DIM:HEAD_DIM + 1, :]
            sq = jnp.sum(o * o, axis=0, keepdims=True)
            for g in range(Q_PER_KV):
                ss = ss + sq[:, g * tq:(g + 1) * tq]
            outs.append(o)
        r = lax.rsqrt(ss * (1.0 / D_ATTN) + RMS_EPS)
        for kh in range(N_KV_HEADS):
            for g in range(Q_PER_KV):
                h = kh * Q_PER_KV + g
                oh = (outs[kh][:, g * tq:(g + 1) * tq] * r).T
                o_ref[:, h * HEAD_DIM:(h + 1) * HEAD_DIM] = (
                    oh * g_ref[:, h * HEAD_DIM:(h + 1) * HEAD_DIM]).astype(BF16)


def _attention(q, k, vt, out_g_attn, batch, seq):
    tq, tk = ATTN_TQ, ATTN_TK
    nq, nk = seq // tq, seq // tk
    m = batch * seq
    return pl.pallas_call(
        _attn_kernel,
        out_shape=jax.ShapeDtypeStruct((m, D_ATTN), BF16),
        grid=(batch, nq, nk),
        in_specs=[
            pl.BlockSpec((N_Q_HEADS, tq, HEAD_DIM), lambda b, i, j: (0, b * nq + i, 0)),
            pl.BlockSpec((N_KV_HEADS, tk, HEAD_DIM), lambda b, i, j: (0, b * nk + j, 0)),
            pl.BlockSpec((N_KV_HEADS, VT_ROWS, tk), lambda b, i, j: (0, 0, b * nk + j)),
            pl.BlockSpec((1, D_ATTN), lambda b, i, j: (0, 0)),
        ],
        out_specs=pl.BlockSpec((tq, D_ATTN), lambda b, i, j: (b * nq + i, 0)),
        scratch_shapes=[
            pltpu.VMEM((SUBLANES, Q_PER_KV * tq), F32),
            pltpu.VMEM((N_KV_HEADS, VT_ROWS, Q_PER_KV * tq), F32),
        ],
        compiler_params=_params(("arbitrary", "arbitrary", "arbitrary")),
        name="attention",
    )(q, k, vt, out_g_attn)


def _gelu_tanh(x):
    return 0.5 * x * (1.0 + jnp.tanh(math.sqrt(2.0 / math.pi) * (x + 0.044715 * (x * x * x))))


def _softplus(x):
    return jnp.maximum(x, 0.0) + jnp.log1p(jnp.exp(-jnp.abs(x)))


def _rglru_kernel(xr_ref, xg_ref, cw_ref, cb_ref, wa_ref, ba_ref, wi_ref, bi_ref, lam_ref,
                  y_ref, xe_s, a_s, u_s):
    seq = xr_ref.shape[0]
    ns = RNN_SEGMENTS
    steps = seq // ns
    t = RNN_CHUNK
    n_chunks = seq // t
    chains = ns // SUBLANES

    cw = cw_ref[...]
    cb = cb_ref[...]
    coef = -RG_C * _softplus(-lam_ref[...])
    w_a = [wa_ref[d].astype(BF16) for d in range(2)]
    w_i = [wi_ref[d].astype(BF16) for d in range(2)]

    seg_id = lax.broadcasted_iota(jnp.int32, (ns, LANES), 0)
    xe_s[pl.ds(0, ns), :] = jnp.where(
        seg_id == 0, 0.0, pltpu.roll(xr_ref[pl.ds(seq - ns, ns), :].astype(F32), 1, 0))
    for k in range(2):
        xe_s[pl.ds(ns + seq + k * ns, ns), :] = jnp.where(
            seg_id == ns - 1, 0.0, pltpu.roll(xr_ref[pl.ds(k * ns, ns), :].astype(F32), ns - 1, 0))

    def stage(c, carry):
        t0 = pl.multiple_of(c * t, t)
        xe_s[pl.ds(ns + t0, t), :] = xr_ref[pl.ds(t0, t), :].astype(F32)
        return carry

    lax.fori_loop(0, n_chunks, stage, 0)

    def gates(c, carry):
        t0 = pl.multiple_of(c * t, t)
        xc = cb
        for tap in range(CONV_W):
            xc = xc + cw[tap:tap + 1, :] * xe_s[pl.ds(t0 + tap * ns, t), :]
        xcb = xc.astype(BF16)
        for d in range(2):
            r = jax.nn.sigmoid(jnp.dot(xcb, w_a[d], preferred_element_type=F32) + ba_ref[d:d + 1, :])
            i = jax.nn.sigmoid(jnp.dot(xcb, w_i[d], preferred_element_type=F32) + bi_ref[d:d + 1, :])
            log_a = coef[d:d + 1, :] * r
            a = jnp.exp(log_a)
            a_s[d, pl.ds(t0, t), :] = a
            u_s[d, pl.ds(t0, t), :] = jnp.sqrt(-jnp.tanh(log_a) * (a * a + 1.0)) * i * xc
        return carry

    lax.fori_loop(0, n_chunks, gates, 0)

    def seg_rows(g, j):
        return pl.ds(pl.multiple_of(j * ns + g * SUBLANES, SUBLANES), SUBLANES)

    def local(j, carry):
        out = []
        for d in range(2):
            jj = j if d == 0 else steps - 1 - j
            for g in range(chains):
                h, p = carry[d * chains + g]
                a = a_s[d, seg_rows(g, jj), :]
                out.append((a * h + u_s[d, seg_rows(g, jj), :], a * p))
        return tuple(out)

    zero = jnp.zeros((SUBLANES, LANES), F32)
    one = jnp.ones((SUBLANES, LANES), F32)
    ends = lax.fori_loop(0, steps, local, tuple((zero, one) for _ in range(2 * chains)), unroll=8)

    starts = []
    for d in range(2):
        h_end = jnp.concatenate([ends[d * chains + g][0] for g in range(chains)], axis=0)
        p_end = jnp.concatenate([ends[d * chains + g][1] for g in range(chains)], axis=0)
        order = range(ns) if d == 0 else range(ns - 1, -1, -1)
        carry = jnp.zeros((1, LANES), F32)
        rows = [None] * ns
        for s in order:
            rows[s] = carry
            carry = p_end[s:s + 1, :] * carry + h_end[s:s + 1, :]
        init = jnp.concatenate(rows, axis=0)
        starts.extend(init[g * SUBLANES:(g + 1) * SUBLANES] for g in range(chains))

    def final(j, carry):
        out = []
        for d in range(2):
            jj = j if d == 0 else steps - 1 - j
            for g in range(chains):
                h = a_s[d, seg_rows(g, jj), :] * carry[d * chains + g] + u_s[d, seg_rows(g, jj), :]
                u_s[d, seg_rows(g, jj), :] = h
                out.append(h)
        return tuple(out)

    lax.fori_loop(0, steps, final, tuple(starts), unroll=8)

    def combine(c, carry):
        t0 = pl.multiple_of(c * t, t)
        h = u_s[0, pl.ds(t0, t), :] + u_s[1, pl.ds(t0, t), :]
        y_ref[pl.ds(t0, t), :] = _gelu_tanh(xg_ref[pl.ds(t0, t), :].astype(F32)) * h
        return carry

    lax.fori_loop(0, n_chunks, combine, 0)


def _rglru(xr, xg, conv_w, conv_b, w_a, b_a, w_i, b_i, lam, batch, seq):
    m, d_rnn = xr.shape
    blk = RNN_BLOCK
    assert seq % RNN_CHUNK == 0 and RNN_CHUNK % RNN_SEGMENTS == 0 and RNN_SEGMENTS % SUBLANES == 0
    slab = pl.BlockSpec((seq, blk), lambda b, n: (b, n))
    vec2 = pl.BlockSpec((2, blk), lambda b, n: (0, n))
    wspec = pl.BlockSpec((2, None, blk, blk), lambda b, n: (0, n, 0, 0))
    return pl.pallas_call(
        _rglru_kernel,
        out_shape=jax.ShapeDtypeStruct((m, d_rnn), F32),
        grid=(batch, d_rnn // blk),
        in_specs=[
            slab, slab,
            pl.BlockSpec((CONV_W, blk), lambda b, n: (0, n)),
            pl.BlockSpec((1, blk), lambda b, n: (0, n)),
            wspec, vec2, wspec, vec2, vec2,
        ],
        out_specs=slab,
        scratch_shapes=[pltpu.VMEM((seq + (CONV_W - 1) * RNN_SEGMENTS, blk), F32),
                        pltpu.VMEM((2, seq, blk), F32), pltpu.VMEM((2, seq, blk), F32)],
        compiler_params=_params(("arbitrary", "arbitrary")),
        name="rg_lru",
    )(xr, xg, conv_w, conv_b.reshape(1, d_rnn), w_a, b_a, w_i, b_i, lam)


def _rms_rows_kernel(y_ref, g_ref, o_ref):
    y = y_ref[...]
    r = lax.rsqrt(jnp.mean(y * y, axis=-1, keepdims=True) + RMS_EPS)
    o_ref[...] = (y * r * g_ref[...]).astype(BF16)


def _rms_rows(y, gain):
    m, d = y.shape
    return pl.pallas_call(
        _rms_rows_kernel,
        out_shape=jax.ShapeDtypeStruct((m, d), BF16),
        grid=(m // ROW_TM,),
        in_specs=[pl.BlockSpec((ROW_TM, d), lambda i: (i, 0)), pl.BlockSpec((1, d), lambda i: (0, 0))],
        out_specs=pl.BlockSpec((ROW_TM, d), lambda i: (i, 0)),
        compiler_params=_params(("arbitrary",)),
        name="rnn_rms",
    )(y, gain)


def _outproj_kernel(a1_ref, a2_ref, w_ref, o_ref):
    k1 = a1_ref.shape[1]
    o_ref[...] = (jnp.dot(a1_ref[...], w_ref[0:k1, :], preferred_element_type=F32)
                  + jnp.dot(a2_ref[...], w_ref[k1:, :], preferred_element_type=F32)).astype(o_ref.dtype)


def _outproj(a1, a2, w_o, layer):
    m, k1 = a1.shape
    k2 = a2.shape[1]
    n = w_o.shape[2]
    return pl.pallas_call(
        _outproj_kernel,
        out_shape=jax.ShapeDtypeStruct((m, n), BF16),
        grid=(m // MM_TM, n // MM_TN),
        in_specs=[
            pl.BlockSpec((MM_TM, k1), lambda i, j: (i, 0)),
            pl.BlockSpec((MM_TM, k2), lambda i, j: (i, 0)),
            pl.BlockSpec((None, k1 + k2, MM_TN), lambda i, j: (layer, 0, j)),
        ],
        out_specs=pl.BlockSpec((MM_TM, MM_TN), lambda i, j: (i, j)),
        compiler_params=_params(("arbitrary", "arbitrary")),
        name="out_proj",
    )(a1, a2, w_o)


def _deepnorm_kernel(x_ref, y_ref, gate_ref, g_ref, b_ref, *rest, alpha, modulate):
    s = alpha * x_ref[...] + gate_ref[...] * y_ref[...].astype(F32)
    mu = jnp.mean(s, axis=-1, keepdims=True)
    sc = s - mu
    var = jnp.mean(sc * sc, axis=-1, keepdims=True)
    out = sc * lax.rsqrt(var + LN_EPS) * g_ref[...] + b_ref[...]
    if modulate:
        sc_ref, sh_ref, o_ref, u_ref = rest
        u_ref[...] = (out * (1.0 + sc_ref[...]) + sh_ref[...]).astype(BF16)
        o_ref[...] = out
    else:
        (o_ref,) = rest
        ns = o_ref.shape[0]
        for jj in range(SUBLANES):
            o_ref[:, jj, :] = out[jj * ns:(jj + 1) * ns, :]


def _deepnorm(x2, y, gate, ln_g, ln_b, nxt, alpha, seq):
    m, d = x2.shape
    per_b = seq // ROW_TM
    rows = pl.BlockSpec((ROW_TM, d), lambda i: (i, 0))
    vec_b = pl.BlockSpec((None, 1, d), lambda i: (i // per_b, 0, 0))
    vec = pl.BlockSpec((1, d), lambda i: (0, 0))
    modulate = nxt is not None
    in_specs = [rows, rows, vec_b, vec, vec]
    args = [x2, y, gate, ln_g.reshape(1, d), ln_b.reshape(1, d)]
    if modulate:
        in_specs += [vec_b, vec_b]
        args += list(nxt)
        out_shape = [jax.ShapeDtypeStruct((m, d), F32), jax.ShapeDtypeStruct((m, d), BF16)]
        out_specs = [rows, rows]
    else:
        ns = RNN_SEGMENTS
        assert ROW_TM == SUBLANES * ns
        out_shape = [jax.ShapeDtypeStruct((m // seq, ns, seq // ns, d), F32)]
        out_specs = [pl.BlockSpec((None, ns, SUBLANES, d), lambda i: (i // per_b, 0, i % per_b, 0))]
    res = pl.pallas_call(
        functools.partial(_deepnorm_kernel, alpha=alpha, modulate=modulate),
        out_shape=tuple(out_shape),
        grid=(m // ROW_TM,),
        in_specs=in_specs,
        out_specs=tuple(out_specs),
        compiler_params=_params(("arbitrary",)),
        name="deepnorm",
    )(*args)
    return res if modulate else (res[0], None)


def _ffn_in_kernel(u_ref, wg_ref, *refs):
    up_refs, h_ref = refs[:-1], refs[-1]
    u = u_ref[...]
    gate = jnp.dot(u, wg_ref[...], preferred_element_type=F32)
    up = jnp.concatenate([jnp.dot(u, w_ref[...], preferred_element_type=F32) for w_ref in up_refs],
                         axis=1)
    h_ref[...] = (gate * jax.nn.sigmoid(gate) * up).astype(BF16)


def _ffn_in(u, w_ffn, layer, d_ff):
    m, d = u.shape
    assert w_ffn.shape[2] == 2 * d_ff and d_ff % FFN_UP_TN == 0 and FFN_TN % FFN_UP_TN == 0
    parts = FFN_TN // FFN_UP_TN
    up0 = d_ff // FFN_UP_TN
    last = 2 * up0 - 1
    up_specs = [pl.BlockSpec((None, d, FFN_UP_TN), functools.partial(
        lambda i, j, p: (layer, 0, jnp.minimum(up0 + parts * j + p, last)), p=p))
                for p in range(parts)]
    return pl.pallas_call(
        _ffn_in_kernel,
        out_shape=jax.ShapeDtypeStruct((m, d_ff), BF16),
        grid=(m // MM_TM, pl.cdiv(d_ff, FFN_TN)),
        in_specs=[pl.BlockSpec((MM_TM, d), lambda i, j: (i, 0)),
                  pl.BlockSpec((None, d, FFN_TN), lambda i, j: (layer, 0, j))] + up_specs,
        out_specs=pl.BlockSpec((MM_TM, FFN_TN), lambda i, j: (i, j)),
        compiler_params=_params(("arbitrary", "arbitrary")),
        name="ffn_in",
    )(u, w_ffn, *([w_ffn] * parts))


def _ffn_down_kernel(h_ref, w_ref, o_ref):
    o_ref[...] = jnp.dot(h_ref[...], w_ref[...], preferred_element_type=F32).astype(o_ref.dtype)


def _ffn_down(h, w_down, layer):
    m, kdim = h.shape
    n = w_down.shape[2]
    return pl.pallas_call(
        _ffn_down_kernel,
        out_shape=jax.ShapeDtypeStruct((m, n), BF16),
        grid=(n // DOWN_TN, m // DOWN_TM),
        in_specs=[
            pl.BlockSpec((DOWN_TM, kdim), lambda j, i: (i, 0)),
            pl.BlockSpec((None, kdim, DOWN_TN), lambda j, i: (layer, 0, j),
                         pipeline_mode=pl.Buffered(1)),
        ],
        out_specs=pl.BlockSpec((DOWN_TM, DOWN_TN), lambda j, i: (i, j)),
        compiler_params=_params(("arbitrary", "arbitrary")),
        name="ffn_down",
    )(h, w_down)


def _rope_tables(seq):
    pos = jnp.arange(seq, dtype=jnp.int32)
    row = (pos // GRID_W).astype(F32)
    col = (pos % GRID_W).astype(F32)
    inv_freq = jnp.power(ROPE_THETA, -jnp.arange(0, ROPE_AXIS_DIM, 2, dtype=F32) / ROPE_AXIS_DIM)
    ang_r = row[:, None] * inv_freq[None, :]
    ang_c = col[:, None] * inv_freq[None, :]
    cos_r, sin_r, cos_c, sin_c = jnp.cos(ang_r), jnp.sin(ang_r), jnp.cos(ang_c), jnp.sin(ang_c)
    zero = jnp.zeros_like(sin_r)
    cos = jnp.concatenate([cos_r, cos_r, cos_c, cos_c], axis=-1)
    sin_a = jnp.concatenate([-sin_r, zero, -sin_c, zero], axis=-1)
    sin_b = jnp.concatenate([zero, sin_r, zero, sin_c], axis=-1)
    return cos, sin_a, sin_b


def kernel(x, c, w_ada, b_ada, w_in, qk_norm_g, conv_w, conv_b, rg_w_a, rg_b_a, rg_w_i, rg_b_i,
           rg_lam, mix_out_g, w_o, ln_g, ln_b, w_ffn_in, w_down):
    batch, seq, d = x.shape
    depth = w_ada.shape[0]
    m = batch * seq
    alpha = (2 * depth) ** 0.25
    assert d == D_MODEL and seq % ATTN_TK == 0 and seq % INPROJ_TM == 0 and m % MM_TM == 0
    assert m % DOWN_TM == 0 and d % DOWN_TN == 0

    ns = RNN_SEGMENTS

    def interleave(a):
        lead = a.shape[:-2]
        return a.reshape(*lead, ns, seq // ns, a.shape[-1]).swapaxes(-3, -2).reshape(a.shape)

    cos, sin_a, sin_b = (interleave(tab) for tab in _rope_tables(seq))
    c_pad = jnp.pad(c, ((0, SUBLANES - batch), (0, 0)))
    mod = _ada(c_pad, w_ada, b_ada)[:, :batch].reshape(depth, batch, 6, 1, d)

    w_in_b = w_in.astype(BF16)
    w_o_b = w_o.astype(BF16)
    w_ffn_b = w_ffn_in.astype(BF16)
    w_down_b = w_down.astype(BF16)

    x2 = interleave(x).reshape(m, d)
    u = _modulate(x2, mod[0, :, 1], mod[0, :, 0], seq)
    for l in range(depth):
        sh2, sc2, g1, g2 = mod[l, :, 3], mod[l, :, 4], mod[l, :, 2], mod[l, :, 5]
        q, k, vt, xr, xg = _inproj(u, w_in_b, l, qk_norm_g[l], cos, sin_a, sin_b, seq)
        y_attn = _attention(q, k, vt, mix_out_g[l, :D_ATTN].reshape(1, D_ATTN), batch, seq)
        y_rnn = _rglru(xr, xg, conv_w[l], conv_b[l], rg_w_a[l], rg_b_a[l], rg_w_i[l], rg_b_i[l],
                       rg_lam[l], batch, seq)
        y_rnn = _rms_rows(y_rnn, mix_out_g[l, D_ATTN:].reshape(1, D_RNN))
        y = _outproj(y_attn, y_rnn, w_o_b, l)
        x2, u = _deepnorm(x2, y, g1, ln_g[l, 0], ln_b[l, 0], (sc2, sh2), alpha, seq)
        h = _ffn_in(u, w_ffn_b, l, D_FF)
        y = _ffn_down(h, w_down_b, l)
        nxt = (mod[l + 1, :, 1], mod[l + 1, :, 0]) if l + 1 < depth else None
        x2, u = _deepnorm(x2, y, g2, ln_g[l, 1], ln_b[l, 1], nxt, alpha, seq)
    return x2.reshape(batch, seq, d)
```

```python
import functools
import math

import jax
import jax.numpy as jnp
from jax import lax
from jax.experimental import pallas as pl
from jax.experimental.pallas import tpu as pltpu

D_MODEL = 4096
GRID_W = 64
HEAD_DIM = 128
D_ATTN = D_MODEL // 2
N_Q_HEADS = D_ATTN // HEAD_DIM
N_KV_HEADS = 4
Q_PER_KV = N_Q_HEADS // N_KV_HEADS
D_KV = N_KV_HEADS * HEAD_DIM
D_RNN = D_MODEL - D_ATTN
RNN_BLOCK = 128
N_RNN_BLOCKS = D_RNN // RNN_BLOCK
D_IN = D_ATTN + 2 * D_KV + 2 * D_RNN
CONV_W = 4
RG_C = 8.0
ROPE_THETA = 10000.0
ROPE_AXIS_DIM = HEAD_DIM // 2
D_FF = (-(-8 * D_MODEL // (3 * 256))) * 256
LN_EPS = 1e-5
RMS_EPS = 1e-6
Q_SCALE = HEAD_DIM ** -0.5 * math.log2(math.e)

V7X_VMEM_BYTES = 64 * 1024 * 1024
VMEM_LIMIT = V7X_VMEM_BYTES - 8 * 1024 * 1024
LANES = 128
SUBLANES = 8

BF16 = jnp.bfloat16
F32 = jnp.float32

ADA_TN = 1024
ADA_K_SPLIT = 4
INPROJ_TM = 1024
INPROJ_TN = 512
INPROJ_ROW_CHUNKS = 4
ATTN_TQ = 512
ATTN_TK = 1024
ATTN_SCORE_SPLIT = 4
ATTN_LOOKAHEAD = 2
VT_ROWS = HEAD_DIM + 16
MM_TM = 1024
MM_TN = 1024
FFN_TN = 512
FFN_UP_TN = 256
DOWN_TM = 512
DOWN_TN = 1024
RNN_SEGMENTS = 32
ROW_TM = 256
RMS_TM = 1024
RNN_CHUNK = 512


def _params(sem):
    return pltpu.CompilerParams(dimension_semantics=sem, vmem_limit_bytes=VMEM_LIMIT)


def _ada_kernel(c_ref, *refs):
    w_refs, (b_ref, o_ref) = refs[:ADA_K_SPLIT], refs[ADA_K_SPLIT:]
    c = c_ref[...]
    c_act = (c * jax.nn.sigmoid(c)).astype(BF16)
    kc = w_refs[0].shape[0]
    acc = b_ref[...]
    for s, w_ref in enumerate(w_refs):
        acc = acc + jnp.dot(c_act[:, s * kc:(s + 1) * kc], w_ref[...].astype(BF16),
                            preferred_element_type=F32)
    o_ref[...] = acc


def _ada(c_pad, w_ada, b_ada):
    depth, d, n = w_ada.shape
    kc = d // ADA_K_SPLIT
    w_specs = [pl.BlockSpec((None, kc, ADA_TN), functools.partial(lambda l, j, s: (l, s, j), s=s))
               for s in range(ADA_K_SPLIT)]
    return pl.pallas_call(
        _ada_kernel,
        out_shape=jax.ShapeDtypeStruct((depth, SUBLANES, n), F32),
        grid=(depth, n // ADA_TN),
        in_specs=[pl.BlockSpec((SUBLANES, d), lambda l, j: (0, 0))] + w_specs
        + [pl.BlockSpec((None, 1, ADA_TN), lambda l, j: (l, 0, j))],
        out_specs=pl.BlockSpec((None, SUBLANES, ADA_TN), lambda l, j: (l, 0, j)),
        compiler_params=_params(("arbitrary", "arbitrary")),
        name="ada_mod",
    )(c_pad, *([w_ada] * ADA_K_SPLIT), b_ada.reshape(depth, 1, n))


def _modulate_kernel(x_ref, sc_ref, sh_ref, u_ref):
    u_ref[...] = (x_ref[...] * (1.0 + sc_ref[...]) + sh_ref[...]).astype(BF16)


def _modulate(x2, sc, sh, seq):
    m, d = x2.shape
    per_b = seq // ROW_TM
    vec = pl.BlockSpec((None, 1, d), lambda i: (i // per_b, 0, 0))
    return pl.pallas_call(
        _modulate_kernel,
        out_shape=jax.ShapeDtypeStruct((m, d), BF16),
        grid=(m // ROW_TM,),
        in_specs=[pl.BlockSpec((ROW_TM, d), lambda i: (i, 0)), vec, vec],
        out_specs=pl.BlockSpec((ROW_TM, d), lambda i: (i, 0)),
        compiler_params=_params(("arbitrary",)),
        name="modulate",
    )(x2, sc, sh)


def _rms_rope(xh, gain, cos, sin_a, sin_b):
    r = lax.rsqrt(jnp.mean(xh * xh, axis=-1, keepdims=True) + RMS_EPS)
    xn = xh * r * gain
    return xn * cos + pltpu.roll(xn, 96, 1) * sin_a + pltpu.roll(xn, 32, 1) * sin_b


_Q_TILES = D_ATTN // INPROJ_TN
_K_TILE = _Q_TILES
_V_TILE = _K_TILE + 1
_R_TILE0 = _V_TILE + 1
_R_TILES = D_RNN // INPROJ_TN
_G_TILE0 = _R_TILE0 + _R_TILES
assert D_KV == INPROJ_TN and D_IN == (_G_TILE0 + _R_TILES) * INPROJ_TN


def _inproj_kernel(u_ref, w_ref, g_ref, cos_ref, sa_ref, sb_ref,
                   q_ref, k_ref, vt_ref, xr_ref, xg_ref):
    j = pl.program_id(1)
    heads = INPROJ_TN // HEAD_DIM
    rows = u_ref.shape[0] // INPROJ_ROW_CHUNKS

    def row_chunks(epilogue):
        def z_chunk(c):
            return jnp.dot(u_ref[c * rows:(c + 1) * rows, :], w_ref[...], preferred_element_type=F32)

        z_next = z_chunk(0)
        for c in range(INPROJ_ROW_CHUNKS):
            z = z_next
            if c + 1 < INPROJ_ROW_CHUNKS:
                z_next = z_chunk(c + 1)
            epilogue(slice(c * rows, (c + 1) * rows), z)

    def head(z, h):
        return z[:, h * HEAD_DIM:(h + 1) * HEAD_DIM]

    def rope_heads(out_ref, gain_row, scale):
        def epilogue(r, z):
            cos, sa, sb = cos_ref[r, :], sa_ref[r, :], sb_ref[r, :]
            gain = g_ref[gain_row:gain_row + 1, :]
            for h in range(heads):
                x = _rms_rope(head(z, h), gain, cos, sa, sb)
                out_ref[h, r, :] = (x if scale is None else x * scale).astype(BF16)
        return epilogue

    @pl.when(j < _Q_TILES)
    def _():
        row_chunks(rope_heads(q_ref, 0, Q_SCALE))

    @pl.when(j == _K_TILE)
    def _():
        row_chunks(rope_heads(k_ref, 1, None))

    @pl.when(j == _V_TILE)
    def _():
        def epilogue(r, z):
            for h in range(heads):
                vt_ref[h, 0:HEAD_DIM, r] = head(z, h).T.astype(BF16)
                vt_ref[h, HEAD_DIM:, r] = jnp.ones((VT_ROWS - HEAD_DIM, rows), BF16)
        row_chunks(epilogue)

    def plain(out_ref):
        def epilogue(r, z):
            out_ref[r, :] = z.astype(BF16)
        return epilogue

    @pl.when((j >= _R_TILE0) & (j < _G_TILE0))
    def _():
        row_chunks(plain(xr_ref))

    @pl.when(j >= _G_TILE0)
    def _():
        row_chunks(plain(xg_ref))


def _inproj(u, w_in, layer, qk_g, cos, sin_a, sin_b, seq):
    m, d = u.shape
    tm, tn = INPROJ_TM, INPROJ_TN
    per_b = seq // tm
    heads = tn // HEAD_DIM
    tab = pl.BlockSpec((tm, HEAD_DIM), lambda i, j: (i % per_b, 0))
    return pl.pallas_call(
        _inproj_kernel,
        out_shape=(
            jax.ShapeDtypeStruct((N_Q_HEADS, m, HEAD_DIM), BF16),
            jax.ShapeDtypeStruct((N_KV_HEADS, m, HEAD_DIM), BF16),
            jax.ShapeDtypeStruct((N_KV_HEADS, VT_ROWS, m), BF16),
            jax.ShapeDtypeStruct((m, D_RNN), BF16),
            jax.ShapeDtypeStruct((m, D_RNN), BF16),
        ),
        grid=(m // tm, D_IN // tn),
        in_specs=[
            pl.BlockSpec((tm, d), lambda i, j: (i, 0)),
            pl.BlockSpec((None, d, tn), lambda i, j: (layer, 0, j)),
            pl.BlockSpec((2, HEAD_DIM), lambda i, j: (0, 0)),
            tab, tab, tab,
        ],
        out_specs=(
            pl.BlockSpec((heads, tm, HEAD_DIM), lambda i, j: (jnp.minimum(j, _Q_TILES - 1), i, 0)),
            pl.BlockSpec((N_KV_HEADS, tm, HEAD_DIM), lambda i, j: (0, i, 0)),
            pl.BlockSpec((N_KV_HEADS, VT_ROWS, tm), lambda i, j: (0, 0, i)),
            pl.BlockSpec((tm, tn), lambda i, j: (i, jnp.clip(j - _R_TILE0, 0, _R_TILES - 1))),
            pl.BlockSpec((tm, tn), lambda i, j: (i, jnp.clip(j - _G_TILE0, 0, _R_TILES - 1))),
        ),
        compiler_params=_params(("arbitrary", "arbitrary")),
        name="in_proj",
    )(u, w_in, qk_g, cos, sin_a, sin_b)


def _attn_kernel(q_ref, k_ref, vt_ref, g_ref, o_ref, m_s, acc_s):
    j = pl.program_id(2)
    tq = q_ref.shape[1]
    tk = k_ref.shape[1]
    rows = tk // ATTN_SCORE_SPLIT

    @pl.when(j == 0)
    def _():
        m_s[...] = jnp.full(m_s.shape, -jnp.inf, F32)
        acc_s[...] = jnp.zeros(acc_s.shape, F32)

    def scores(kh, g):
        q = q_ref[Q_PER_KV * kh + g]
        return [lax.dot_general(k_ref[kh, r * rows:(r + 1) * rows, :], q, (((1,), (1,)), ((), ())),
                                preferred_element_type=F32) for r in range(ATTN_SCORE_SPLIT)]

    def update(kh, g, s_parts):
        cols = slice(g * tq, (g + 1) * tq)
        m_old = m_s[kh:kh + 1, cols]
        m_new = m_old
        for s_t in s_parts:
            m_new = jnp.maximum(m_new, jnp.max(s_t, axis=0, keepdims=True))
        alpha = jnp.exp2(m_old - m_new)
        p = jnp.concatenate([jnp.exp2(s_t - m_new).astype(BF16) for s_t in s_parts], axis=0)
        pv = jnp.dot(vt_ref[kh], p, preferred_element_type=F32)
        acc_s[kh, :, cols] = alpha * acc_s[kh, :, cols] + pv
        m_s[kh:kh + 1, cols] = m_new

    blocks = [(kh, g) for kh in range(N_KV_HEADS) for g in range(Q_PER_KV)]
    pending = [scores(*blk) for blk in blocks[:ATTN_LOOKAHEAD]]
    for c, (kh, g) in enumerate(blocks):
        if c + ATTN_LOOKAHEAD < len(blocks):
            pending.append(scores(*blocks[c + ATTN_LOOKAHEAD]))
        update(kh, g, pending.pop(0))

    @pl.when(j == pl.num_programs(2) - 1)
    def _():
        outs = []
        ss = jnp.zeros((1, tq), F32)
        for kh in range(N_KV_HEADS):
            o = acc_s[kh, 0:HEAD_DIM, :] * (1.0 / acc_s[kh, HEAD_DIM:HEAD_DIM + 1, :])
            sq = jnp.sum(o * o, axis=0, keepdims=True)
            for g in range(Q_PER_KV):
                ss = ss + sq[:, g * tq:(g + 1) * tq]
            outs.append(o)
        r = lax.rsqrt(ss * (1.0 / D_ATTN) + RMS_EPS)
        for kh in range(N_KV_HEADS):
            for g in range(Q_PER_KV):
                h = kh * Q_PER_KV + g
                oh = (outs[kh][:, g * tq:(g + 1) * tq] * r).T
                o_ref[:, h * HEAD_DIM:(h + 1) * HEAD_DIM] = (
                    oh * g_ref[:, h * HEAD_DIM:(h + 1) * HEAD_DIM]).astype(BF16)


def _attention(q, k, vt, out_g_attn, batch, seq):
    tq, tk = ATTN_TQ, ATTN_TK
    nq, nk = seq // tq, seq // tk
    m = batch * seq
    return pl.pallas_call(
        _attn_kernel,
        out_shape=jax.ShapeDtypeStruct((m, D_ATTN), BF16),
        grid=(batch, nq, nk),
        in_specs=[
            pl.BlockSpec((N_Q_HEADS, tq, HEAD_DIM), lambda b, i, j: (0, b * nq + i, 0)),
            pl.BlockSpec((N_KV_HEADS, tk, HEAD_DIM), lambda b, i, j: (0, b * nk + j, 0)),
            pl.BlockSpec((N_KV_HEADS, VT_ROWS, tk), lambda b, i, j: (0, 0, b * nk + j)),
            pl.BlockSpec((1, D_ATTN), lambda b, i, j: (0, 0)),
        ],
        out_specs=pl.BlockSpec((tq, D_ATTN), lambda b, i, j: (b * nq + i, 0)),
        scratch_shapes=[
            pltpu.VMEM((SUBLANES, Q_PER_KV * tq), F32),
            pltpu.VMEM((N_KV_HEADS, VT_ROWS, Q_PER_KV * tq), F32),
        ],
        compiler_params=_params(("arbitrary", "arbitrary", "arbitrary")),
        name="attention",
    )(q, k, vt, out_g_attn)


def _gelu_tanh(x):
    return 0.5 * x * (1.0 + jnp.tanh(math.sqrt(2.0 / math.pi) * (x + 0.044715 * (x * x * x))))


def _softplus(x):
    return jnp.maximum(x, 0.0) + jnp.log1p(jnp.exp(-jnp.abs(x)))


def _rglru_kernel(xr_ref, xg_ref, cw_ref, cb_ref, wa_ref, ba_ref, wi_ref, bi_ref, lam_ref,
                  y_ref, xe_s, a_s, u_s):
    seq = xr_ref.shape[0]
    ns = RNN_SEGMENTS
    steps = seq // ns
    t = RNN_CHUNK
    n_chunks = seq // t
    chains = ns // SUBLANES

    cw = cw_ref[...]
    cb = cb_ref[...]
    coef = -RG_C * _softplus(-lam_ref[...])
    w_a = [wa_ref[d].astype(BF16) for d in range(2)]
    w_i = [wi_ref[d].astype(BF16) for d in range(2)]

    seg_id = lax.broadcasted_iota(jnp.int32, (ns, LANES), 0)
    xe_s[pl.ds(0, ns), :] = jnp.where(
        seg_id == 0, 0.0, pltpu.roll(xr_ref[pl.ds(seq - ns, ns), :].astype(F32), 1, 0))
    for k in range(2):
        xe_s[pl.ds(ns + seq + k * ns, ns), :] = jnp.where(
            seg_id == ns - 1, 0.0, pltpu.roll(xr_ref[pl.ds(k * ns, ns), :].astype(F32), ns - 1, 0))

    def stage(c, carry):
        t0 = pl.multiple_of(c * t, t)
        xe_s[pl.ds(ns + t0, t), :] = xr_ref[pl.ds(t0, t), :].astype(F32)
        return carry

    lax.fori_loop(0, n_chunks, stage, 0)

    def gates(c, carry):
        t0 = pl.multiple_of(c * t, t)
        xc = cb
        for tap in range(CONV_W):
            xc = xc + cw[tap:tap + 1, :] * xe_s[pl.ds(t0 + tap * ns, t), :]
        xcb = xc.astype(BF16)
        for d in range(2):
            r = jax.nn.sigmoid(jnp.dot(xcb, w_a[d], preferred_element_type=F32) + ba_ref[d:d + 1, :])
            i = jax.nn.sigmoid(jnp.dot(xcb, w_i[d], preferred_element_type=F32) + bi_ref[d:d + 1, :])
            log_a = coef[d:d + 1, :] * r
            a = jnp.exp(log_a)
            a_s[d, pl.ds(t0, t), :] = a
            u_s[d, pl.ds(t0, t), :] = jnp.sqrt(-jnp.tanh(log_a) * (a * a + 1.0)) * i * xc
        return carry

    lax.fori_loop(0, n_chunks, gates, 0)

    def seg_rows(g, j):
        return pl.ds(pl.multiple_of(j * ns + g * SUBLANES, SUBLANES), SUBLANES)

    def local(j, carry):
        out = []
        for d in range(2):
            jj = j if d == 0 else steps - 1 - j
            for g in range(chains):
                h, p = carry[d * chains + g]
                a = a_s[d, seg_rows(g, jj), :]
                out.append((a * h + u_s[d, seg_rows(g, jj), :], a * p))
        return tuple(out)

    zero = jnp.zeros((SUBLANES, LANES), F32)
    one = jnp.ones((SUBLANES, LANES), F32)
    ends = lax.fori_loop(0, steps, local, tuple((zero, one) for _ in range(2 * chains)), unroll=8)

    starts = []
    for d in range(2):
        h_end = jnp.concatenate([ends[d * chains + g][0] for g in range(chains)], axis=0)
        p_end = jnp.concatenate([ends[d * chains + g][1] for g in range(chains)], axis=0)
        order = range(ns) if d == 0 else range(ns - 1, -1, -1)
        carry = jnp.zeros((1, LANES), F32)
        rows = [None] * ns
        for s in order:
            rows[s] = carry
            carry = p_end[s:s + 1, :] * carry + h_end[s:s + 1, :]
        init = jnp.concatenate(rows, axis=0)
        starts.extend(init[g * SUBLANES:(g + 1) * SUBLANES] for g in range(chains))

    def final(j, carry):
        out = []
        for d in range(2):
            jj = j if d == 0 else steps - 1 - j
            for g in range(chains):
                h = a_s[d, seg_rows(g, jj), :] * carry[d * chains + g] + u_s[d, seg_rows(g, jj), :]
                u_s[d, seg_rows(g, jj), :] = h
                out.append(h)
        return tuple(out)

    lax.fori_loop(0, steps, final, tuple(starts), unroll=8)

    def combine(c, carry):
        t0 = pl.multiple_of(c * t, t)
        h = u_s[0, pl.ds(t0, t), :] + u_s[1, pl.ds(t0, t), :]
        y_ref[pl.ds(t0, t), :] = (_gelu_tanh(xg_ref[pl.ds(t0, t), :].astype(F32)) * h).astype(y_ref.dtype)
        return carry

    lax.fori_loop(0, n_chunks, combine, 0)


def _rglru(xr, xg, conv_w, conv_b, w_a, b_a, w_i, b_i, lam, batch, seq):
    m, d_rnn = xr.shape
    blk = RNN_BLOCK
    assert seq % RNN_CHUNK == 0 and RNN_CHUNK % RNN_SEGMENTS == 0 and RNN_SEGMENTS % SUBLANES == 0
    slab = pl.BlockSpec((seq, blk), lambda b, n: (b, n))
    vec2 = pl.BlockSpec((2, blk), lambda b, n: (0, n))
    wspec = pl.BlockSpec((2, None, blk, blk), lambda b, n: (0, n, 0, 0))
    return pl.pallas_call(
        _rglru_kernel,
        out_shape=jax.ShapeDtypeStruct((m, d_rnn), BF16),
        grid=(batch, d_rnn // blk),
        in_specs=[
            slab, slab,
            pl.BlockSpec((CONV_W, blk), lambda b, n: (0, n)),
            pl.BlockSpec((1, blk), lambda b, n: (0, n)),
            wspec, vec2, wspec, vec2, vec2,
        ],
        out_specs=slab,
        scratch_shapes=[pltpu.VMEM((seq + (CONV_W - 1) * RNN_SEGMENTS, blk), F32),
                        pltpu.VMEM((2, seq, blk), F32), pltpu.VMEM((2, seq, blk), F32)],
        compiler_params=_params(("arbitrary", "arbitrary")),
        name="rg_lru",
    )(xr, xg, conv_w, conv_b.reshape(1, d_rnn), w_a, b_a, w_i, b_i, lam)


def _rms_rows_kernel(y_ref, g_ref, o_ref):
    y = y_ref[...].astype(F32)
    r = lax.rsqrt(jnp.mean(y * y, axis=-1, keepdims=True) + RMS_EPS)
    o_ref[...] = (y * r * g_ref[...]).astype(BF16)


def _rms_rows(y, gain):
    m, d = y.shape
    return pl.pallas_call(
        _rms_rows_kernel,
        out_shape=jax.ShapeDtypeStruct((m, d), BF16),
        grid=(m // RMS_TM,),
        in_specs=[pl.BlockSpec((RMS_TM, d), lambda i: (i, 0)), pl.BlockSpec((1, d), lambda i: (0, 0))],
        out_specs=pl.BlockSpec((RMS_TM, d), lambda i: (i, 0)),
        compiler_params=_params(("arbitrary",)),
        name="rnn_rms",
    )(y, gain)


def _outproj_kernel(a1_ref, a2_ref, w_ref, o_ref):
    k1 = a1_ref.shape[1]
    o_ref[...] = (jnp.dot(a1_ref[...], w_ref[0:k1, :], preferred_element_type=F32)
                  + jnp.dot(a2_ref[...], w_ref[k1:, :], preferred_element_type=F32)).astype(o_ref.dtype)


def _outproj(a1, a2, w_o, layer):
    m, k1 = a1.shape
    k2 = a2.shape[1]
    n = w_o.shape[2]
    return pl.pallas_call(
        _outproj_kernel,
        out_shape=jax.ShapeDtypeStruct((m, n), BF16),
        grid=(m // MM_TM, n // MM_TN),
        in_specs=[
            pl.BlockSpec((MM_TM, k1), lambda i, j: (i, 0)),
            pl.BlockSpec((MM_TM, k2), lambda i, j: (i, 0)),
            pl.BlockSpec((None, k1 + k2, MM_TN), lambda i, j: (layer, 0, j)),
        ],
        out_specs=pl.BlockSpec((MM_TM, MM_TN), lambda i, j: (i, j)),
        compiler_params=_params(("arbitrary", "arbitrary")),
        name="out_proj",
    )(a1, a2, w_o)


def _deepnorm_kernel(x_ref, y_ref, gate_ref, g_ref, b_ref, *rest, alpha, modulate):
    s = alpha * x_ref[...] + gate_ref[...] * y_ref[...].astype(F32)
    mu = jnp.mean(s, axis=-1, keepdims=True)
    sc = s - mu
    var = jnp.mean(sc * sc, axis=-1, keepdims=True)
    out = sc * lax.rsqrt(var + LN_EPS) * g_ref[...] + b_ref[...]
    if modulate:
        sc_ref, sh_ref, o_ref, u_ref = rest
        u_ref[...] = (out * (1.0 + sc_ref[...]) + sh_ref[...]).astype(BF16)
        o_ref[...] = out
    else:
        (o_ref,) = rest
        ns = o_ref.shape[0]
        for jj in range(SUBLANES):
            o_ref[:, jj, :] = out[jj * ns:(jj + 1) * ns, :]


def _deepnorm(x2, y, gate, ln_g, ln_b, nxt, alpha, seq):
    m, d = x2.shape
    per_b = seq // ROW_TM
    rows = pl.BlockSpec((ROW_TM, d), lambda i: (i, 0))
    vec_b = pl.BlockSpec((None, 1, d), lambda i: (i // per_b, 0, 0))
    vec = pl.BlockSpec((1, d), lambda i: (0, 0))
    modulate = nxt is not None
    in_specs = [rows, rows, vec_b, vec, vec]
    args = [x2, y, gate, ln_g.reshape(1, d), ln_b.reshape(1, d)]
    if modulate:
        in_specs += [vec_b, vec_b]
        args += list(nxt)
        out_shape = [jax.ShapeDtypeStruct((m, d), F32), jax.ShapeDtypeStruct((m, d), BF16)]
        out_specs = [rows, rows]
    else:
        ns = RNN_SEGMENTS
        assert ROW_TM == SUBLANES * ns
        out_shape = [jax.ShapeDtypeStruct((m // seq, ns, seq // ns, d), F32)]
        out_specs = [pl.BlockSpec((None, ns, SUBLANES, d), lambda i: (i // per_b, 0, i % per_b, 0))]
    res = pl.pallas_call(
        functools.partial(_deepnorm_kernel, alpha=alpha, modulate=modulate),
        out_shape=tuple(out_shape),
        grid=(m // ROW_TM,),
        in_specs=in_specs,
        out_specs=tuple(out_specs),
        compiler_params=_params(("arbitrary",)),
        name="deepnorm",
    )(*args)
    return res if modulate else (res[0], None)


def _ffn_in_kernel(u_ref, wg_ref, *refs):
    up_refs, h_ref = refs[:-1], refs[-1]
    u = u_ref[...]
    gate = jnp.dot(u, wg_ref[...], preferred_element_type=F32)
    up = jnp.concatenate([jnp.dot(u, w_ref[...], preferred_element_type=F32) for w_ref in up_refs],
                         axis=1)
    h_ref[...] = (gate * jax.nn.sigmoid(gate) * up).astype(BF16)


def _ffn_in(u, w_ffn, layer, d_ff):
    m, d = u.shape
    assert w_ffn.shape[2] == 2 * d_ff and d_ff % FFN_UP_TN == 0 and FFN_TN % FFN_UP_TN == 0
    parts = FFN_TN // FFN_UP_TN
    up0 = d_ff // FFN_UP_TN
    last = 2 * up0 - 1
    up_specs = [pl.BlockSpec((None, d, FFN_UP_TN), functools.partial(
        lambda i, j, p: (layer, 0, jnp.minimum(up0 + parts * j + p, last)), p=p))
                for p in range(parts)]
    return pl.pallas_call(
        _ffn_in_kernel,
        out_shape=jax.ShapeDtypeStruct((m, d_ff), BF16),
        grid=(m // MM_TM, pl.cdiv(d_ff, FFN_TN)),
        in_specs=[pl.BlockSpec((MM_TM, d), lambda i, j: (i, 0)),
                  pl.BlockSpec((None, d, FFN_TN), lambda i, j: (layer, 0, j))] + up_specs,
        out_specs=pl.BlockSpec((MM_TM, FFN_TN), lambda i, j: (i, j)),
        compiler_params=_params(("arbitrary", "arbitrary")),
        name="ffn_in",
    )(u, w_ffn, *([w_ffn] * parts))


def _ffn_down_kernel(h_ref, w_ref, o_ref):
    o_ref[...] = jnp.dot(h_ref[...], w_ref[...], preferred_element_type=F32).astype(o_ref.dtype)


def _ffn_down(h, w_down, layer):
    m, kdim = h.shape
    n = w_down.shape[2]
    return pl.pallas_call(
        _ffn_down_kernel,
        out_shape=jax.ShapeDtypeStruct((m, n), BF16),
        grid=(n // DOWN_TN, m // DOWN_TM),
        in_specs=[
            pl.BlockSpec((DOWN_TM, kdim), lambda j, i: (i, 0)),
            pl.BlockSpec((None, kdim, DOWN_TN), lambda j, i: (layer, 0, j),
                         pipeline_mode=pl.Buffered(1)),
        ],
        out_specs=pl.BlockSpec((DOWN_TM, DOWN_TN), lambda j, i: (i, j)),
        compiler_params=_params(("arbitrary", "arbitrary")),
        name="ffn_down",
    )(h, w_down)


def _rope_tables(seq):
    pos = jnp.arange(seq, dtype=jnp.int32)
    row = (pos // GRID_W).astype(F32)
    col = (pos % GRID_W).astype(F32)
    inv_freq = jnp.power(ROPE_THETA, -jnp.arange(0, ROPE_AXIS_DIM, 2, dtype=F32) / ROPE_AXIS_DIM)
    ang_r = row[:, None] * inv_freq[None, :]
    ang_c = col[:, None] * inv_freq[None, :]
    cos_r, sin_r, cos_c, sin_c = jnp.cos(ang_r), jnp.sin(ang_r), jnp.cos(ang_c), jnp.sin(ang_c)
    zero = jnp.zeros_like(sin_r)
    cos = jnp.concatenate([cos_r, cos_r, cos_c, cos_c], axis=-1)
    sin_a = jnp.concatenate([-sin_r, zero, -sin_c, zero], axis=-1)
    sin_b = jnp.concatenate([zero, sin_r, zero, sin_c], axis=-1)
    return cos, sin_a, sin_b


def kernel(x, c, w_ada, b_ada, w_in, qk_norm_g, conv_w, conv_b, rg_w_a, rg_b_a, rg_w_i, rg_b_i,
           rg_lam, mix_out_g, w_o, ln_g, ln_b, w_ffn_in, w_down):
    batch, seq, d = x.shape
    depth = w_ada.shape[0]
    m = batch * seq
    alpha = (2 * depth) ** 0.25
    assert d == D_MODEL and seq % ATTN_TK == 0 and seq % INPROJ_TM == 0 and m % MM_TM == 0
    assert m % DOWN_TM == 0 and d % DOWN_TN == 0

    ns = RNN_SEGMENTS

    def interleave(a):
        lead = a.shape[:-2]
        return a.reshape(*lead, ns, seq // ns, a.shape[-1]).swapaxes(-3, -2).reshape(a.shape)

    cos, sin_a, sin_b = (interleave(tab) for tab in _rope_tables(seq))
    c_pad = jnp.pad(c, ((0, SUBLANES - batch), (0, 0)))
    mod = _ada(c_pad, w_ada, b_ada)[:, :batch].reshape(depth, batch, 6, 1, d)

    w_in_b = w_in.astype(BF16)
    w_o_b = w_o.astype(BF16)
    w_ffn_b = w_ffn_in.astype(BF16)
    w_down_b = w_down.astype(BF16)

    x2 = interleave(x).reshape(m, d)
    u = _modulate(x2, mod[0, :, 1], mod[0, :, 0], seq)
    for l in range(depth):
        sh2, sc2, g1, g2 = mod[l, :, 3], mod[l, :, 4], mod[l, :, 2], mod[l, :, 5]
        q, k, vt, xr, xg = _inproj(u, w_in_b, l, qk_norm_g[l], cos, sin_a, sin_b, seq)
        y_attn = _attention(q, k, vt, mix_out_g[l, :D_ATTN].reshape(1, D_ATTN), batch, seq)
        y_rnn = _rglru(xr, xg, conv_w[l], conv_b[l], rg_w_a[l], rg_b_a[l], rg_w_i[l], rg_b_i[l],
                       rg_lam[l], batch, seq)
        y_rnn = _rms_rows(y_rnn, mix_out_g[l, D_ATTN:].reshape(1, D_RNN))
        y = _outproj(y_attn, y_rnn, w_o_b, l)
        x2, u = _deepnorm(x2, y, g1, ln_g[l, 0], ln_b[l, 0], (sc2, sh2), alpha, seq)
        h = _ffn_in(u, w_ffn_b, l, D_FF)
        y = _ffn_down(h, w_down_b, l)
        nxt = (mod[l + 1, :, 1], mod[l + 1, :, 0]) if l + 1 < depth else None
        x2, u = _deepnorm(x2, y, g2, ln_g[l, 1], ln_b[l, 1], nxt, alpha, seq)
    return x2.reshape(batch, seq, d)
```

```python
import functools
import math

import jax
import jax.numpy as jnp
from jax import lax
from jax.experimental import pallas as pl
from jax.experimental.pallas import tpu as pltpu

D_MODEL = 4096
GRID_W = 64
HEAD_DIM = 128
D_ATTN = D_MODEL // 2
N_Q_HEADS = D_ATTN // HEAD_DIM
N_KV_HEADS = 4
Q_PER_KV = N_Q_HEADS // N_KV_HEADS
D_KV = N_KV_HEADS * HEAD_DIM
D_RNN = D_MODEL - D_ATTN
RNN_BLOCK = 128
N_RNN_BLOCKS = D_RNN // RNN_BLOCK
D_IN = D_ATTN + 2 * D_KV + 2 * D_RNN
CONV_W = 4
RG_C = 8.0
ROPE_THETA = 10000.0
ROPE_AXIS_DIM = HEAD_DIM // 2
D_FF = (-(-8 * D_MODEL // (3 * 256))) * 256
LN_EPS = 1e-5
RMS_EPS = 1e-6
Q_SCALE = HEAD_DIM ** -0.5 * math.log2(math.e)

V7X_VMEM_BYTES = 64 * 1024 * 1024
VMEM_LIMIT = V7X_VMEM_BYTES - 8 * 1024 * 1024
LANES = 128
SUBLANES = 8

BF16 = jnp.bfloat16
F32 = jnp.float32

ADA_TN = 1024
ADA_K_SPLIT = 4
INPROJ_TM = 1024
INPROJ_TN = 512
INPROJ_ROW_CHUNKS = 4
ATTN_TQ = 512
ATTN_TK = 1024
ATTN_SCORE_SPLIT = 4
ATTN_LOOKAHEAD = 2
VT_ROWS = HEAD_DIM + 16
MM_TM = 1024
MM_TN = 1024
FFN_TN = 512
FFN_UP_TN = 256
DOWN_TM = 512
DOWN_TN = 1024
RNN_SEGMENTS = 32
ROW_TM = 256
RMS_TM = 1024
RNN_CHUNK = 512


def _params(sem):
    return pltpu.CompilerParams(dimension_semantics=sem, vmem_limit_bytes=VMEM_LIMIT)


def _ada_kernel(c_ref, *refs):
    w_refs, (b_ref, o_ref) = refs[:ADA_K_SPLIT], refs[ADA_K_SPLIT:]
    c = c_ref[...]
    c_act = (c * jax.nn.sigmoid(c)).astype(BF16)
    kc = w_refs[0].shape[0]
    acc = b_ref[...]
    for s, w_ref in enumerate(w_refs):
        acc = acc + jnp.dot(c_act[:, s * kc:(s + 1) * kc], w_ref[...].astype(BF16),
                            preferred_element_type=F32)
    o_ref[...] = acc


def _ada(c_pad, w_ada, b_ada):
    depth, d, n = w_ada.shape
    kc = d // ADA_K_SPLIT
    w_specs = [pl.BlockSpec((None, kc, ADA_TN), functools.partial(lambda l, j, s: (l, s, j), s=s))
               for s in range(ADA_K_SPLIT)]
    return pl.pallas_call(
        _ada_kernel,
        out_shape=jax.ShapeDtypeStruct((depth, SUBLANES, n), F32),
        grid=(depth, n // ADA_TN),
        in_specs=[pl.BlockSpec((SUBLANES, d), lambda l, j: (0, 0))] + w_specs
        + [pl.BlockSpec((None, 1, ADA_TN), lambda l, j: (l, 0, j))],
        out_specs=pl.BlockSpec((None, SUBLANES, ADA_TN), lambda l, j: (l, 0, j)),
        compiler_params=_params(("arbitrary", "arbitrary")),
        name="ada_mod",
    )(c_pad, *([w_ada] * ADA_K_SPLIT), b_ada.reshape(depth, 1, n))


def _modulate_kernel(x_ref, sc_ref, sh_ref, u_ref):
    u_ref[...] = (x_ref[...] * (1.0 + sc_ref[...]) + sh_ref[...]).astype(BF16)


def _modulate(x2, sc, sh, seq):
    m, d = x2.shape
    per_b = seq // ROW_TM
    vec = pl.BlockSpec((None, 1, d), lambda i: (i // per_b, 0, 0))
    return pl.pallas_call(
        _modulate_kernel,
        out_shape=jax.ShapeDtypeStruct((m, d), BF16),
        grid=(m // ROW_TM,),
        in_specs=[pl.BlockSpec((ROW_TM, d), lambda i: (i, 0)), vec, vec],
        out_specs=pl.BlockSpec((ROW_TM, d), lambda i: (i, 0)),
        compiler_params=_params(("arbitrary",)),
        name="modulate",
    )(x2, sc, sh)


def _rms_rope(xh, gain, cos, sin_a, sin_b):
    r = lax.rsqrt(jnp.mean(xh * xh, axis=-1, keepdims=True) + RMS_EPS)
    xn = xh * r * gain
    return xn * cos + pltpu.roll(xn, 96, 1) * sin_a + pltpu.roll(xn, 32, 1) * sin_b


_Q_TILES = D_ATTN // INPROJ_TN
_K_TILE = _Q_TILES
_V_TILE = _K_TILE + 1
_R_TILE0 = _V_TILE + 1
_R_TILES = D_RNN // INPROJ_TN
_G_TILE0 = _R_TILE0 + _R_TILES
assert D_KV == INPROJ_TN and D_IN == (_G_TILE0 + _R_TILES) * INPROJ_TN


def _inproj_kernel(u_ref, w_ref, g_ref, cos_ref, sa_ref, sb_ref,
                   q_ref, k_ref, vt_ref, xr_ref, xg_ref):
    j = pl.program_id(1)
    heads = INPROJ_TN // HEAD_DIM
    rows = u_ref.shape[0] // INPROJ_ROW_CHUNKS

    def row_chunks(epilogue):
        def z_chunk(c):
            return jnp.dot(u_ref[c * rows:(c + 1) * rows, :], w_ref[...], preferred_element_type=F32)

        z_next = z_chunk(0)
        for c in range(INPROJ_ROW_CHUNKS):
            z = z_next
            if c + 1 < INPROJ_ROW_CHUNKS:
                z_next = z_chunk(c + 1)
            epilogue(slice(c * rows, (c + 1) * rows), z)

    def head(z, h):
        return z[:, h * HEAD_DIM:(h + 1) * HEAD_DIM]

    def rope_heads(out_ref, gain_row, scale):
        def epilogue(r, z):
            cos, sa, sb = cos_ref[r, :], sa_ref[r, :], sb_ref[r, :]
            gain = g_ref[gain_row:gain_row + 1, :]
            for h in range(heads):
                x = _rms_rope(head(z, h), gain, cos, sa, sb)
                out_ref[h, r, :] = (x if scale is None else x * scale).astype(BF16)
        return epilogue

    @pl.when(j < _Q_TILES)
    def _():
        row_chunks(rope_heads(q_ref, 0, Q_SCALE))

    @pl.when(j == _K_TILE)
    def _():
        row_chunks(rope_heads(k_ref, 1, None))

    @pl.when(j == _V_TILE)
    def _():
        def epilogue(r, z):
            for h in range(heads):
                vt_ref[h, 0:HEAD_DIM, r] = head(z, h).T.astype(BF16)
                vt_ref[h, HEAD_DIM:, r] = jnp.ones((VT_ROWS - HEAD_DIM, rows), BF16)
        row_chunks(epilogue)

    def plain(out_ref):
        def epilogue(r, z):
            out_ref[r, :] = z.astype(BF16)
        return epilogue

    @pl.when((j >= _R_TILE0) & (j < _G_TILE0))
    def _():
        row_chunks(plain(xr_ref))

    @pl.when(j >= _G_TILE0)
    def _():
        row_chunks(plain(xg_ref))


def _inproj(u, w_in, layer, qk_g, cos, sin_a, sin_b, seq):
    m, d = u.shape
    tm, tn = INPROJ_TM, INPROJ_TN
    per_b = seq // tm
    heads = tn // HEAD_DIM
    tab = pl.BlockSpec((tm, HEAD_DIM), lambda i, j: (i % per_b, 0))
    return pl.pallas_call(
        _inproj_kernel,
        out_shape=(
            jax.ShapeDtypeStruct((N_Q_HEADS, m, HEAD_DIM), BF16),
            jax.ShapeDtypeStruct((N_KV_HEADS, m, HEAD_DIM), BF16),
            jax.ShapeDtypeStruct((N_KV_HEADS, VT_ROWS, m), BF16),
            jax.ShapeDtypeStruct((m, D_RNN), BF16),
            jax.ShapeDtypeStruct((m, D_RNN), BF16),
        ),
        grid=(m // tm, D_IN // tn),
        in_specs=[
            pl.BlockSpec((tm, d), lambda i, j: (i, 0)),
            pl.BlockSpec((None, d, tn), lambda i, j: (layer, 0, j)),
            pl.BlockSpec((2, HEAD_DIM), lambda i, j: (0, 0)),
            tab, tab, tab,
        ],
        out_specs=(
            pl.BlockSpec((heads, tm, HEAD_DIM), lambda i, j: (jnp.minimum(j, _Q_TILES - 1), i, 0)),
            pl.BlockSpec((N_KV_HEADS, tm, HEAD_DIM), lambda i, j: (0, i, 0)),
            pl.BlockSpec((N_KV_HEADS, VT_ROWS, tm), lambda i, j: (0, 0, i)),
            pl.BlockSpec((tm, tn), lambda i, j: (i, jnp.clip(j - _R_TILE0, 0, _R_TILES - 1))),
            pl.BlockSpec((tm, tn), lambda i, j: (i, jnp.clip(j - _G_TILE0, 0, _R_TILES - 1))),
        ),
        compiler_params=_params(("arbitrary", "arbitrary")),
        name="in_proj",
    )(u, w_in, qk_g, cos, sin_a, sin_b)


def _attn_kernel(q_ref, k_ref, vt_ref, g_ref, o_ref, m_s, acc_s):
    j = pl.program_id(2)
    tq = q_ref.shape[1]
    tk = k_ref.shape[1]
    rows = tk // ATTN_SCORE_SPLIT

    @pl.when(j == 0)
    def _():
        m_s[...] = jnp.full(m_s.shape, -jnp.inf, F32)
        acc_s[...] = jnp.zeros(acc_s.shape, F32)

    def scores(kh, g):
        q = q_ref[Q_PER_KV * kh + g]
        return [lax.dot_general(k_ref[kh, r * rows:(r + 1) * rows, :], q, (((1,), (1,)), ((), ())),
                                preferred_element_type=F32) for r in range(ATTN_SCORE_SPLIT)]

    def update(kh, g, s_parts):
        cols = slice(g * tq, (g + 1) * tq)
        m_old = m_s[kh:kh + 1, cols]
        m_new = m_old
        for s_t in s_parts:
            m_new = jnp.maximum(m_new, jnp.max(s_t, axis=0, keepdims=True))
        alpha = jnp.exp2(m_old - m_new)
        p = jnp.concatenate([jnp.exp2(s_t - m_new).astype(BF16) for s_t in s_parts], axis=0)
        pv = jnp.dot(vt_ref[kh], p, preferred_element_type=F32)
        acc_s[kh, :, cols] = alpha * acc_s[kh, :, cols] + pv
        m_s[kh:kh + 1, cols] = m_new

    blocks = [(kh, g) for kh in range(N_KV_HEADS) for g in range(Q_PER_KV)]
    pending = [scores(*blk) for blk in blocks[:ATTN_LOOKAHEAD]]
    for c, (kh, g) in enumerate(blocks):
        if c + ATTN_LOOKAHEAD < len(blocks):
            pending.append(scores(*blocks[c + ATTN_LOOKAHEAD]))
        update(kh, g, pending.pop(0))

    @pl.when(j == pl.num_programs(2) - 1)
    def _():
        outs = []
        ss = jnp.zeros((1, tq), F32)
        for kh in range(N_KV_HEADS):
            o = acc_s[kh, 0:HEAD_DIM, :] * (1.0 / acc_s[kh, HEAD_DIM:HEAD_DIM + 1, :])
            sq = jnp.sum(o * o, axis=0, keepdims=True)
            for g in range(Q_PER_KV):
                ss = ss + sq[:, g * tq:(g + 1) * tq]
            outs.append(o)
        r = lax.rsqrt(ss * (1.0 / D_ATTN) + RMS_EPS)
        for kh in range(N_KV_HEADS):
            for g in range(Q_PER_KV):
                h = kh * Q_PER_KV + g
                oh = (outs[kh][:, g * tq:(g + 1) * tq] * r).T
                o_ref[:, h * HEAD_DIM:(h + 1) * HEAD_DIM] = (
                    oh * g_ref[:, h * HEAD_DIM:(h + 1) * HEAD_DIM]).astype(BF16)


def _attention(q, k, vt, out_g_attn, batch, seq):
    tq, tk = ATTN_TQ, ATTN_TK
    nq, nk = seq // tq, seq // tk
    m = batch * seq
    return pl.pallas_call(
        _attn_kernel,
        out_shape=jax.ShapeDtypeStruct((m, D_ATTN), BF16),
        grid=(batch, nq, nk),
        in_specs=[
            pl.BlockSpec((N_Q_HEADS, tq, HEAD_DIM), lambda b, i, j: (0, b * nq + i, 0)),
            pl.BlockSpec((N_KV_HEADS, tk, HEAD_DIM), lambda b, i, j: (0, b * nk + j, 0)),
            pl.BlockSpec((N_KV_HEADS, VT_ROWS, tk), lambda b, i, j: (0, 0, b * nk + j)),
            pl.BlockSpec((1, D_ATTN), lambda b, i, j: (0, 0)),
        ],
        out_specs=pl.BlockSpec((tq, D_ATTN), lambda b, i, j: (b * nq + i, 0)),
        scratch_shapes=[
            pltpu.VMEM((SUBLANES, Q_PER_KV * tq), F32),
            pltpu.VMEM((N_KV_HEADS, VT_ROWS, Q_PER_KV * tq), F32),
        ],
        compiler_params=_params(("arbitrary", "arbitrary", "arbitrary")),
        name="attention",
    )(q, k, vt, out_g_attn)


def _gelu_tanh(x):
    return 0.5 * x * (1.0 + jnp.tanh(math.sqrt(2.0 / math.pi) * (x + 0.044715 * (x * x * x))))


def _softplus(x):
    return jnp.maximum(x, 0.0) + jnp.log1p(jnp.exp(-jnp.abs(x)))


def _rglru_kernel(xr_ref, xg_ref, cw_ref, cb_ref, wa_ref, ba_ref, wi_ref, bi_ref, lam_ref,
                  y_ref, xe_s, a_s, u_s):
    seq = xr_ref.shape[0]
    ns = RNN_SEGMENTS
    steps = seq // ns
    t = RNN_CHUNK
    n_chunks = seq // t
    chains = ns // SUBLANES

    cw = cw_ref[...]
    cb = cb_ref[...]
    coef = -RG_C * _softplus(-lam_ref[...])
    w_a = [wa_ref[d].astype(BF16) for d in range(2)]
    w_i = [wi_ref[d].astype(BF16) for d in range(2)]

    seg_id = lax.broadcasted_iota(jnp.int32, (ns, LANES), 0)
    xe_s[pl.ds(0, ns), :] = jnp.where(
        seg_id == 0, 0.0, pltpu.roll(xr_ref[pl.ds(seq - ns, ns), :].astype(F32), 1, 0))
    for k in range(2):
        xe_s[pl.ds(ns + seq + k * ns, ns), :] = jnp.where(
            seg_id == ns - 1, 0.0, pltpu.roll(xr_ref[pl.ds(k * ns, ns), :].astype(F32), ns - 1, 0))

    def stage(c, carry):
        t0 = pl.multiple_of(c * t, t)
        xe_s[pl.ds(ns + t0, t), :] = xr_ref[pl.ds(t0, t), :].astype(F32)
        return carry

    lax.fori_loop(0, n_chunks, stage, 0)

    def gates(c, carry):
        t0 = pl.multiple_of(c * t, t)
        xc = cb
        for tap in range(CONV_W):
            xc = xc + cw[tap:tap + 1, :] * xe_s[pl.ds(t0 + tap * ns, t), :]
        xcb = xc.astype(BF16)
        for d in range(2):
            r = jax.nn.sigmoid(jnp.dot(xcb, w_a[d], preferred_element_type=F32) + ba_ref[d:d + 1, :])
            i = jax.nn.sigmoid(jnp.dot(xcb, w_i[d], preferred_element_type=F32) + bi_ref[d:d + 1, :])
            log_a = coef[d:d + 1, :] * r
            a = jnp.exp(log_a)
            a_s[d, pl.ds(t0, t), :] = a
            u_s[d, pl.ds(t0, t), :] = jnp.sqrt(-jnp.tanh(log_a) * (a * a + 1.0)) * i * xc
        return carry

    lax.fori_loop(0, n_chunks, gates, 0)

    def seg_rows(g, j):
        return pl.ds(pl.multiple_of(j * ns + g * SUBLANES, SUBLANES), SUBLANES)

    def local(j, carry):
        out = []
        for d in range(2):
            jj = j if d == 0 else steps - 1 - j
            for g in range(chains):
                h, p = carry[d * chains + g]
                a = a_s[d, seg_rows(g, jj), :]
                out.append((a * h + u_s[d, seg_rows(g, jj), :], a * p))
        return tuple(out)

    zero = jnp.zeros((SUBLANES, LANES), F32)
    one = jnp.ones((SUBLANES, LANES), F32)
    ends = lax.fori_loop(0, steps, local, tuple((zero, one) for _ in range(2 * chains)), unroll=8)

    starts = []
    for d in range(2):
        h_end = jnp.concatenate([ends[d * chains + g][0] for g in range(chains)], axis=0)
        p_end = jnp.concatenate([ends[d * chains + g][1] for g in range(chains)], axis=0)
        order = range(ns) if d == 0 else range(ns - 1, -1, -1)
        carry = jnp.zeros((1, LANES), F32)
        rows = [None] * ns
        for s in order:
            rows[s] = carry
            carry = p_end[s:s + 1, :] * carry + h_end[s:s + 1, :]
        init = jnp.concatenate(rows, axis=0)
        starts.extend(init[g * SUBLANES:(g + 1) * SUBLANES] for g in range(chains))

    def final(j, carry):
        out = []
        for d in range(2):
            jj = j if d == 0 else steps - 1 - j
            for g in range(chains):
                h = a_s[d, seg_rows(g, jj), :] * carry[d * chains + g] + u_s[d, seg_rows(g, jj), :]
                u_s[d, seg_rows(g, jj), :] = h
                out.append(h)
        return tuple(out)

    lax.fori_loop(0, steps, final, tuple(starts), unroll=8)

    def combine(c, carry):
        t0 = pl.multiple_of(c * t, t)
        h = u_s[0, pl.ds(t0, t), :] + u_s[1, pl.ds(t0, t), :]
        y_ref[pl.ds(t0, t), :] = (_gelu_tanh(xg_ref[pl.ds(t0, t), :].astype(F32)) * h).astype(y_ref.dtype)
        return carry

    lax.fori_loop(0, n_chunks, combine, 0)


def _rglru(xr, xg, conv_w, conv_b, w_a, b_a, w_i, b_i, lam, batch, seq):
    m, d_rnn = xr.shape
    blk = RNN_BLOCK
    assert seq % RNN_CHUNK == 0 and RNN_CHUNK % RNN_SEGMENTS == 0 and RNN_SEGMENTS % SUBLANES == 0
    slab = pl.BlockSpec((seq, blk), lambda b, n: (b, n))
    vec2 = pl.BlockSpec((2, blk), lambda b, n: (0, n))
    wspec = pl.BlockSpec((2, None, blk, blk), lambda b, n: (0, n, 0, 0))
    return pl.pallas_call(
        _rglru_kernel,
        out_shape=jax.ShapeDtypeStruct((m, d_rnn), BF16),
        grid=(batch, d_rnn // blk),
        in_specs=[
            slab, slab,
            pl.BlockSpec((CONV_W, blk), lambda b, n: (0, n)),
            pl.BlockSpec((1, blk), lambda b, n: (0, n)),
            wspec, vec2, wspec, vec2, vec2,
        ],
        out_specs=slab,
        scratch_shapes=[pltpu.VMEM((seq + (CONV_W - 1) * RNN_SEGMENTS, blk), F32),
                        pltpu.VMEM((2, seq, blk), F32), pltpu.VMEM((2, seq, blk), F32)],
        compiler_params=_params(("arbitrary", "arbitrary")),
        name="rg_lru",
    )(xr, xg, conv_w, conv_b.reshape(1, d_rnn), w_a, b_a, w_i, b_i, lam)


def _rms_rows_kernel(y_ref, g_ref, o_ref):
    y = y_ref[...].astype(F32)
    r = lax.rsqrt(jnp.mean(y * y, axis=-1, keepdims=True) + RMS_EPS)
    o_ref[...] = (y * r * g_ref[...]).astype(BF16)


def _rms_rows(y, gain):
    m, d = y.shape
    return pl.pallas_call(
        _rms_rows_kernel,
        out_shape=jax.ShapeDtypeStruct((m, d), BF16),
        grid=(m // RMS_TM,),
        in_specs=[pl.BlockSpec((RMS_TM, d), lambda i: (i, 0)), pl.BlockSpec((1, d), lambda i: (0, 0))],
        out_specs=pl.BlockSpec((RMS_TM, d), lambda i: (i, 0)),
        compiler_params=_params(("arbitrary",)),
        name="rnn_rms",
    )(y, gain)


def _outproj_kernel(a1_ref, a2_ref, w_ref, o_ref):
    k1 = a1_ref.shape[1]
    o_ref[...] = (jnp.dot(a1_ref[...], w_ref[0:k1, :], preferred_element_type=F32)
                  + jnp.dot(a2_ref[...], w_ref[k1:, :], preferred_element_type=F32)).astype(o_ref.dtype)


def _outproj(a1, a2, w_o, layer):
    m, k1 = a1.shape
    k2 = a2.shape[1]
    n = w_o.shape[2]
    return pl.pallas_call(
        _outproj_kernel,
        out_shape=jax.ShapeDtypeStruct((m, n), BF16),
        grid=(m // MM_TM, n // MM_TN),
        in_specs=[
            pl.BlockSpec((MM_TM, k1), lambda i, j: (i, 0)),
            pl.BlockSpec((MM_TM, k2), lambda i, j: (i, 0)),
            pl.BlockSpec((None, k1 + k2, MM_TN), lambda i, j: (layer, 0, j)),
        ],
        out_specs=pl.BlockSpec((MM_TM, MM_TN), lambda i, j: (i, j)),
        compiler_params=_params(("arbitrary", "arbitrary")),
        name="out_proj",
    )(a1, a2, w_o)


def _deepnorm_kernel(x_ref, y_ref, gate_ref, g_ref, b_ref, *rest, alpha, modulate):
    s = alpha * x_ref[...] + gate_ref[...] * y_ref[...].astype(F32)
    mu = jnp.mean(s, axis=-1, keepdims=True)
    sc = s - mu
    var = jnp.mean(sc * sc, axis=-1, keepdims=True)
    out = sc * lax.rsqrt(var + LN_EPS) * g_ref[...] + b_ref[...]
    if modulate:
        sc_ref, sh_ref, o_ref, u_ref = rest
        u_ref[...] = (out * (1.0 + sc_ref[...]) + sh_ref[...]).astype(BF16)
        o_ref[...] = out
    else:
        (o_ref,) = rest
        ns = o_ref.shape[0]
        for jj in range(SUBLANES):
            o_ref[:, jj, :] = out[jj * ns:(jj + 1) * ns, :]


def _deepnorm(x2, y, gate, ln_g, ln_b, nxt, alpha, seq):
    m, d = x2.shape
    per_b = seq // ROW_TM
    rows = pl.BlockSpec((ROW_TM, d), lambda i: (i, 0))
    vec_b = pl.BlockSpec((None, 1, d), lambda i: (i // per_b, 0, 0))
    vec = pl.BlockSpec((1, d), lambda i: (0, 0))
    modulate = nxt is not None
    in_specs = [rows, rows, vec_b, vec, vec]
    args = [x2, y, gate, ln_g.reshape(1, d), ln_b.reshape(1, d)]
    if modulate:
        in_specs += [vec_b, vec_b]
        args += list(nxt)
        out_shape = [jax.ShapeDtypeStruct((m, d), F32), jax.ShapeDtypeStruct((m, d), BF16)]
        out_specs = [rows, rows]
    else:
        ns = RNN_SEGMENTS
        assert ROW_TM == SUBLANES * ns
        out_shape = [jax.ShapeDtypeStruct((m // seq, ns, seq // ns, d), F32)]
        out_specs = [pl.BlockSpec((None, ns, SUBLANES, d), lambda i: (i // per_b, 0, i % per_b, 0))]
    res = pl.pallas_call(
        functools.partial(_deepnorm_kernel, alpha=alpha, modulate=modulate),
        out_shape=tuple(out_shape),
        grid=(m // ROW_TM,),
        in_specs=in_specs,
        out_specs=tuple(out_specs),
        compiler_params=_params(("arbitrary",)),
        name="deepnorm",
    )(*args)
    return res if modulate else (res[0], None)


def _ffn_in_kernel(u_ref, wg_ref, *refs, d_ff):
    up_refs, h_ref = refs[:-1], refs[-1]
    u = u_ref[...]

    def swiglu(gate, up):
        return (gate * jax.nn.sigmoid(gate) * up).astype(BF16)

    def up_dot(w_ref):
        return jnp.dot(u, w_ref[...], preferred_element_type=F32)

    tail = (d_ff % FFN_TN) // FFN_UP_TN
    full = pl.program_id(1) < pl.num_programs(1) - 1 if tail else None

    def full_block():
        gate = jnp.dot(u, wg_ref[...], preferred_element_type=F32)
        h_ref[...] = swiglu(gate, jnp.concatenate([up_dot(w_ref) for w_ref in up_refs], axis=1))

    if full is None:
        full_block()
    else:
        pl.when(full)(full_block)

        @pl.when(jnp.logical_not(full))
        def _():
            cols = tail * FFN_UP_TN
            gate = jnp.dot(u, wg_ref[:, 0:cols], preferred_element_type=F32)
            up = jnp.concatenate([up_dot(w_ref) for w_ref in up_refs[:tail]], axis=1)
            h_ref[:, 0:cols] = swiglu(gate, up)


def _ffn_in(u, w_ffn, layer, d_ff):
    m, d = u.shape
    assert w_ffn.shape[2] == 2 * d_ff and d_ff % FFN_UP_TN == 0 and FFN_TN % FFN_UP_TN == 0
    parts = FFN_TN // FFN_UP_TN
    up0 = d_ff // FFN_UP_TN
    last = 2 * up0 - 1
    up_specs = [pl.BlockSpec((None, d, FFN_UP_TN), functools.partial(
        lambda i, j, p: (layer, 0, jnp.minimum(up0 + parts * j + p, last)), p=p))
                for p in range(parts)]
    return pl.pallas_call(
        functools.partial(_ffn_in_kernel, d_ff=d_ff),
        out_shape=jax.ShapeDtypeStruct((m, d_ff), BF16),
        grid=(m // MM_TM, pl.cdiv(d_ff, FFN_TN)),
        in_specs=[pl.BlockSpec((MM_TM, d), lambda i, j: (i, 0)),
                  pl.BlockSpec((None, d, FFN_TN), lambda i, j: (layer, 0, j))] + up_specs,
        out_specs=pl.BlockSpec((MM_TM, FFN_TN), lambda i, j: (i, j)),
        compiler_params=_params(("arbitrary", "arbitrary")),
        name="ffn_in",
    )(u, w_ffn, *([w_ffn] * parts))


def _ffn_down_kernel(h_ref, w_ref, o_ref):
    o_ref[...] = jnp.dot(h_ref[...], w_ref[...], preferred_element_type=F32).astype(o_ref.dtype)


def _ffn_down(h, w_down, layer):
    m, kdim = h.shape
    n = w_down.shape[2]
    return pl.pallas_call(
        _ffn_down_kernel,
        out_shape=jax.ShapeDtypeStruct((m, n), BF16),
        grid=(n // DOWN_TN, m // DOWN_TM),
        in_specs=[
            pl.BlockSpec((DOWN_TM, kdim), lambda j, i: (i, 0)),
            pl.BlockSpec((None, kdim, DOWN_TN), lambda j, i: (layer, 0, j),
                         pipeline_mode=pl.Buffered(1)),
        ],
        out_specs=pl.BlockSpec((DOWN_TM, DOWN_TN), lambda j, i: (i, j)),
        compiler_params=_params(("arbitrary", "arbitrary")),
        name="ffn_down",
    )(h, w_down)


def _rope_tables(seq):
    pos = jnp.arange(seq, dtype=jnp.int32)
    row = (pos // GRID_W).astype(F32)
    col = (pos % GRID_W).astype(F32)
    inv_freq = jnp.power(ROPE_THETA, -jnp.arange(0, ROPE_AXIS_DIM, 2, dtype=F32) / ROPE_AXIS_DIM)
    ang_r = row[:, None] * inv_freq[None, :]
    ang_c = col[:, None] * inv_freq[None, :]
    cos_r, sin_r, cos_c, sin_c = jnp.cos(ang_r), jnp.sin(ang_r), jnp.cos(ang_c), jnp.sin(ang_c)
    zero = jnp.zeros_like(sin_r)
    cos = jnp.concatenate([cos_r, cos_r, cos_c, cos_c], axis=-1)
    sin_a = jnp.concatenate([-sin_r, zero, -sin_c, zero], axis=-1)
    sin_b = jnp.concatenate([zero, sin_r, zero, sin_c], axis=-1)
    return cos, sin_a, sin_b


def kernel(x, c, w_ada, b_ada, w_in, qk_norm_g, conv_w, conv_b, rg_w_a, rg_b_a, rg_w_i, rg_b_i,
           rg_lam, mix_out_g, w_o, ln_g, ln_b, w_ffn_in, w_down):
    batch, seq, d = x.shape
    depth = w_ada.shape[0]
    m = batch * seq
    alpha = (2 * depth) ** 0.25
    assert d == D_MODEL and seq % ATTN_TK == 0 and seq % INPROJ_TM == 0 and m % MM_TM == 0
    assert m % DOWN_TM == 0 and d % DOWN_TN == 0

    ns = RNN_SEGMENTS

    def interleave(a):
        lead = a.shape[:-2]
        return a.reshape(*lead, ns, seq // ns, a.shape[-1]).swapaxes(-3, -2).reshape(a.shape)

    cos, sin_a, sin_b = (interleave(tab) for tab in _rope_tables(seq))
    c_pad = jnp.pad(c, ((0, SUBLANES - batch), (0, 0)))
    mod = _ada(c_pad, w_ada, b_ada)[:, :batch].reshape(depth, batch, 6, 1, d)

    w_in_b = w_in.astype(BF16)
    w_o_b = w_o.astype(BF16)
    w_ffn_b = w_ffn_in.astype(BF16)
    w_down_b = w_down.astype(BF16)

    x2 = interleave(x).reshape(m, d)
    u = _modulate(x2, mod[0, :, 1], mod[0, :, 0], seq)
    for l in range(depth):
        sh2, sc2, g1, g2 = mod[l, :, 3], mod[l, :, 4], mod[l, :, 2], mod[l, :, 5]
        q, k, vt, xr, xg = _inproj(u, w_in_b, l, qk_norm_g[l], cos, sin_a, sin_b, seq)
        y_attn = _attention(q, k, vt, mix_out_g[l, :D_ATTN].reshape(1, D_ATTN), batch, seq)
        y_rnn = _rglru(xr, xg, conv_w[l], conv_b[l], rg_w_a[l], rg_b_a[l], rg_w_i[l], rg_b_i[l],
                       rg_lam[l], batch, seq)
        y_rnn = _rms_rows(y_rnn, mix_out_g[l, D_ATTN:].reshape(1, D_RNN))
        y = _outproj(y_attn, y_rnn, w_o_b, l)
        x2, u = _deepnorm(x2, y, g1, ln_g[l, 0], ln_b[l, 0], (sc2, sh2), alpha, seq)
        h = _ffn_in(u, w_ffn_b, l, D_FF)
        y = _ffn_down(h, w_down_b, l)
        nxt = (mod[l + 1, :, 1], mod[l + 1, :, 0]) if l + 1 < depth else None
        x2, u = _deepnorm(x2, y, g2, ln_g[l, 1], ln_b[l, 1], nxt, alpha, seq)
    return x2.reshape(batch, seq, d)
```
